```python
import jax, jax.numpy as jnp
from jax import lax
import numpy as np

D_MODEL = 2048
BATCH = 2
SEQ = 8192
DEPTH = 4

MIX_WIDTH = D_MODEL
BLOCK = 128
WINDOW = 128
SWA_HEAD_DIM = 64
SWA_Q_HEADS = (MIX_WIDTH // 2) // SWA_HEAD_DIM
SWA_GROUP = 4
SWA_KV_HEADS = SWA_Q_HEADS // SWA_GROUP
MLA_NOPE_DIM = 128
MLA_ROPE_DIM = 64
MLA_QK_DIM = MLA_NOPE_DIM + MLA_ROPE_DIM
MLA_V_DIM = 128
MLA_HEADS = (MIX_WIDTH // 2) // MLA_V_DIM
MLA_Q_RANK = D_MODEL // 4
MLA_KV_RANK = D_MODEL // 8
ROPE_THETA = 10000.0
D_FF = 5632
PLE_DIM = 256
EPS = 1e-6

SWA_Q_COLS = SWA_Q_HEADS * SWA_HEAD_DIM
SWA_KV_COLS = SWA_KV_HEADS * SWA_HEAD_DIM
_SIZES = (SWA_Q_COLS, SWA_KV_COLS, SWA_KV_COLS, MLA_Q_RANK, MLA_KV_RANK, MLA_ROPE_DIM)
IN_COLS = sum(_SIZES)
IN_SPLITS = tuple(int(v) for v in np.cumsum(_SIZES)[:-1])

kernel_name = "hybrid_swa_sink_mla_macaron_ple"


def rms_norm(x, g):
    xf = x.astype(jnp.float32)
    y = xf * lax.rsqrt(jnp.mean(xf * xf, axis=-1, keepdims=True) + EPS)
    return (y * g.astype(jnp.float32)).astype(x.dtype)


def swiglu(x, w_gate, w_up, w_down):
    return (jax.nn.silu(x @ w_gate) * (x @ w_up)) @ w_down


def alibi_slopes(n_heads):
    h = np.arange(1, n_heads + 1, dtype=np.float32)
    return jnp.asarray(2.0 ** (-8.0 * h / n_heads), dtype=jnp.float32)


def apply_rope(x, positions):
    half = x.shape[-1] // 2
    inv_freq = ROPE_THETA ** (-jnp.arange(half, dtype=jnp.float32) / half)
    ang = positions.astype(jnp.float32)[:, :, None] * inv_freq
    cos = jnp.cos(ang)[:, :, None, :]
    sin = jnp.sin(ang)[:, :, None, :]
    xf = x.astype(jnp.float32)
    x1, x2 = xf[..., :half], xf[..., half:]
    return jnp.concatenate([x1 * cos - x2 * sin, x2 * cos + x1 * sin], axis=-1).astype(x.dtype)


def _with_prev_block(t, nb):
    tb = t.reshape(t.shape[0], nb, BLOCK, *t.shape[2:])
    prev = jnp.pad(tb[:, :-1], [(0, 0), (1, 0)] + [(0, 0)] * (tb.ndim - 2))
    return jnp.concatenate([prev, tb], axis=2)


def sliding_window_attention(q, k, v, positions, sinks):
    B, S = q.shape[0], q.shape[1]
    nb = S // BLOCK
    qb = q.reshape(B, nb, BLOCK, SWA_KV_HEADS, SWA_GROUP, SWA_HEAD_DIM)
    kb = _with_prev_block(k, nb)
    vb = _with_prev_block(v, nb)
    pk = _with_prev_block(positions, nb)
    pq = positions.reshape(B, nb, BLOCK)
    s = jnp.einsum('bnqhgd,bnkhd->bnhgqk', qb, kb,
                   preferred_element_type=jnp.float32) * (SWA_HEAD_DIM ** -0.5)
    dist = jnp.abs(pq[..., :, None] - pk[..., None, :]).astype(jnp.float32)
    slopes = alibi_slopes(SWA_Q_HEADS).reshape(SWA_KV_HEADS, SWA_GROUP)
    s = s - slopes[None, None, :, :, None, None] * dist[:, :, None, None]
    qi = jnp.arange(BLOCK)[:, None]
    kj = jnp.arange(2 * BLOCK)[None, :]
    diff = qi + BLOCK - kj
    band = (diff >= 0) & (diff < WINDOW)
    exists = (jnp.arange(nb)[:, None, None] > 0) | (kj[None] >= BLOCK)
    valid = band[None] & exists
    s = jnp.where(valid[None, :, None, None], s, -jnp.inf)
    sink = sinks.astype(jnp.float32).reshape(SWA_KV_HEADS, SWA_GROUP)[None, None, :, :, None, None]
    m = jnp.maximum(jnp.max(s, axis=-1, keepdims=True), sink)
    e = jnp.exp(s - m)
    probs = e / (jnp.sum(e, axis=-1, keepdims=True) + jnp.exp(sink - m))
    o = jnp.einsum('bnhgqk,bnkhd->bnqhgd', probs.astype(v.dtype), vb)
    return o.reshape(B, S, SWA_Q_HEADS * SWA_HEAD_DIM)


def causal_block_attention(q, k, v):
    B, S = q.shape[0], q.shape[1]
    nb = S // BLOCK
    qb = jnp.moveaxis(q.reshape(B, nb, BLOCK, MLA_HEADS, MLA_QK_DIM), 1, 0)
    kidx = jnp.arange(S)
    scale = MLA_QK_DIM ** -0.5

    def one_block(args):
        qblk, n = args
        s = jnp.einsum('bqhd,bkhd->bhqk', qblk, k, preferred_element_type=jnp.float32) * scale
        qidx = n * BLOCK + jnp.arange(BLOCK)
        s = jnp.where(kidx[None, :] <= qidx[:, None], s, -jnp.inf)
        probs = jax.nn.softmax(s, axis=-1)
        return jnp.einsum('bhqk,bkhd->bqhd', probs.astype(v.dtype), v)

    o = lax.map(one_block, (qb, jnp.arange(nb)))
    return jnp.moveaxis(o, 0, 1).reshape(B, S, MLA_HEADS * MLA_V_DIM)


def setup_inputs(seed: int = 0) -> dict:
    key = jax.random.key(seed)
    ks = iter(jax.random.split(key, 40))
    f32 = jnp.float32

    def w(shape, fan_in):
        return jax.random.normal(next(ks), shape, f32) * (fan_in ** -0.5)

    def gain(dim):
        return 1.0 + 0.05 * jax.random.normal(next(ks), (DEPTH, dim), f32)

    x = jax.random.normal(next(ks), (BATCH, SEQ, D_MODEL), f32)
    p = jax.random.normal(next(ks), (DEPTH, BATCH, SEQ, PLE_DIM), f32)
    offset = jax.random.randint(next(ks), (BATCH, 1), 0, 1024, dtype=jnp.int32)
    positions = (offset + jnp.arange(SEQ, dtype=jnp.int32)[None, :]).astype(jnp.int32)
    return {
        "x": x,
        "p": p,
        "positions": positions,
        "ffn1_norm": gain(D_MODEL),
        "ffn1_w_gate": w((DEPTH, D_MODEL, D_FF), D_MODEL),
        "ffn1_w_up": w((DEPTH, D_MODEL, D_FF), D_MODEL),
        "ffn1_w_down": w((DEPTH, D_FF, D_MODEL), D_FF),
        "mix_norm": gain(D_MODEL),
        "w_in": w((DEPTH, D_MODEL, IN_COLS), D_MODEL),
        "swa_q_norm": gain(SWA_HEAD_DIM),
        "swa_k_norm": gain(SWA_HEAD_DIM),
        "swa_sinks": 0.5 * jax.random.normal(next(ks), (DEPTH, SWA_Q_HEADS), f32),
        "mla_q_lora_norm": gain(MLA_Q_RANK),
        "mla_w_uq": w((DEPTH, MLA_Q_RANK, MLA_HEADS * MLA_QK_DIM), MLA_Q_RANK),
        "mla_kv_lora_norm": gain(MLA_KV_RANK),
        "mla_w_ukv": w((DEPTH, MLA_KV_RANK, MLA_HEADS * (MLA_NOPE_DIM + MLA_V_DIM)), MLA_KV_RANK),
        "mla_q_norm": gain(MLA_QK_DIM),
        "mla_k_norm": gain(MLA_QK_DIM),
        "out_norm_swa": gain(SWA_Q_COLS),
        "out_norm_mla": gain(MLA_HEADS * MLA_V_DIM),
        "w_o": w((DEPTH, MIX_WIDTH, D_MODEL), MIX_WIDTH),
        "ffn2_norm": gain(D_MODEL),
        "ffn2_w_gate": w((DEPTH, D_MODEL, D_FF), D_MODEL),
        "ffn2_w_up": w((DEPTH, D_MODEL, D_FF), D_MODEL),
        "ffn2_w_down": w((DEPTH, D_FF, D_MODEL), D_FF),
        "ple_proj": w((DEPTH, PLE_DIM, D_MODEL), PLE_DIM),
        "ple_proj_norm": gain(D_MODEL),
        "ple_gate_norm": gain(D_MODEL),
        "ple_gate": w((DEPTH, D_MODEL, D_MODEL), D_MODEL),
    }


def reference(x, p, positions, ffn1_norm, ffn1_w_gate, ffn1_w_up, ffn1_w_down,
              mix_norm, w_in, swa_q_norm, swa_k_norm, swa_sinks,
              mla_q_lora_norm, mla_w_uq, mla_kv_lora_norm, mla_w_ukv, mla_q_norm, mla_k_norm,
              out_norm_swa, out_norm_mla, w_o,
              ffn2_norm, ffn2_w_gate, ffn2_w_up, ffn2_w_down,
              ple_proj, ple_proj_norm, ple_gate_norm, ple_gate):
    B, S = x.shape[0], x.shape[1]
    for i in range(DEPTH):
        h = x + 0.5 * swiglu(rms_norm(x, ffn1_norm[i]), ffn1_w_gate[i], ffn1_w_up[i], ffn1_w_down[i])

        n = rms_norm(h, mix_norm[i])
        z = n @ w_in[i]
        q_a, k_a, v_a, c_q, c_kv, k_r = jnp.split(z, IN_SPLITS, axis=-1)

        q_a = rms_norm(q_a.reshape(B, S, SWA_Q_HEADS, SWA_HEAD_DIM), swa_q_norm[i])
        k_a = rms_norm(k_a.reshape(B, S, SWA_KV_HEADS, SWA_HEAD_DIM), swa_k_norm[i])
        v_a = v_a.reshape(B, S, SWA_KV_HEADS, SWA_HEAD_DIM)
        o_a = sliding_window_attention(q_a, k_a, v_a, positions, swa_sinks[i])

        q_b = (rms_norm(c_q, mla_q_lora_norm[i]) @ mla_w_uq[i]).reshape(B, S, MLA_HEADS, MLA_QK_DIM)
        kv = (rms_norm(c_kv, mla_kv_lora_norm[i]) @ mla_w_ukv[i]).reshape(
            B, S, MLA_HEADS, MLA_NOPE_DIM + MLA_V_DIM)
        k_nope, v_b = kv[..., :MLA_NOPE_DIM], kv[..., MLA_NOPE_DIM:]
        k_rope = jnp.broadcast_to(k_r[:, :, None, :], (B, S, MLA_HEADS, MLA_ROPE_DIM))
        k_b = jnp.concatenate([k_nope, k_rope], axis=-1)
        q_b = rms_norm(q_b, mla_q_norm[i])
        k_b = rms_norm(k_b, mla_k_norm[i])
        q_b = jnp.concatenate([q_b[..., :MLA_NOPE_DIM], apply_rope(q_b[..., MLA_NOPE_DIM:], positions)], axis=-1)
        k_b = jnp.concatenate([k_b[..., :MLA_NOPE_DIM], apply_rope(k_b[..., MLA_NOPE_DIM:], positions)], axis=-1)
        o_b = causal_block_attention(q_b, k_b, v_b)

        mixed = jnp.concatenate([rms_norm(o_a, out_norm_swa[i]), rms_norm(o_b, out_norm_mla[i])], axis=-1)
        h = h + mixed @ w_o[i]

        h = h + 0.5 * swiglu(rms_norm(h, ffn2_norm[i]), ffn2_w_gate[i], ffn2_w_up[i], ffn2_w_down[i])

        gate = jax.nn.sigmoid(rms_norm(h, ple_gate_norm[i]) @ ple_gate[i])
        x = h + rms_norm(p[i] @ ple_proj[i], ple_proj_norm[i]) * gate
    return x
```

```python
import functools

import numpy as np
import jax
import jax.numpy as jnp
from jax import lax
from jax.experimental import pallas as pl
from jax.experimental.pallas import tpu as pltpu

F32 = jnp.float32
BF16 = jnp.bfloat16

EPS = 1e-6
ROPE_THETA = 10000.0
LANES = 128
SWA_BLOCK = 128
SWA_HEAD_DIM = 64
SWA_Q_HEADS = 16
SWA_KV_HEADS = 4
SWA_GROUP = 4
MLA_HEADS = 8
MLA_NOPE = 128
MLA_ROPE = 64
MLA_QK = MLA_NOPE + MLA_ROPE
MLA_V = 128
MLA_Q_RANK = 512
MLA_KV_RANK = 256
MLA_QK_PAD = 2 * LANES
VMEM_LIMIT = 56 * 1024 * 1024

NT_DIMS = (((1,), (1,)), ((), ()))


def _rms(x, g):
    ms = jnp.mean(x * x, axis=-1, keepdims=True)
    return x * lax.rsqrt(ms + EPS) * g


def _params(sem, vmem=VMEM_LIMIT):
    return pltpu.CompilerParams(dimension_semantics=sem, vmem_limit_bytes=vmem)


def _ffn_kernel(x_ref, g_ref, wg_ref, wu_ref, wd_ref, o_ref, xn_ref):
    f = pl.program_id(1)
    last = pl.num_programs(1) - 1

    @pl.when(f == 0)
    def _():
        xn_ref[...] = _rms(x_ref[...], g_ref[...]).astype(BF16)

    xn = xn_ref[...]
    gate = jnp.dot(xn, wg_ref[...], preferred_element_type=F32)
    up = jnp.dot(xn, wu_ref[...], preferred_element_type=F32)
    h = (gate / (1.0 + jnp.exp(-gate)) * up).astype(BF16)
    contrib = jnp.dot(h, wd_ref[...], preferred_element_type=F32)

    @pl.when(f == 0)
    def _():
        o_ref[...] = contrib

    @pl.when(jnp.logical_and(f > 0, f < last))
    def _():
        o_ref[...] += contrib

    @pl.when(f == last)
    def _():
        o_ref[...] = x_ref[...] + 0.5 * (o_ref[...] + contrib)


def _ffn(x, g, wg, wu, wd, *, tm=1024, tf=512):
    T, D = x.shape
    Fd = wg.shape[1]
    assert T % tm == 0 and Fd % tf == 0
    return pl.pallas_call(
        _ffn_kernel,
        grid=(T // tm, Fd // tf),
        in_specs=[
            pl.BlockSpec((tm, D), lambda i, f: (i, 0), pipeline_mode=pl.Buffered(1)),
            pl.BlockSpec((1, D), lambda i, f: (0, 0)),
            pl.BlockSpec((D, tf), lambda i, f: (0, f)),
            pl.BlockSpec((D, tf), lambda i, f: (0, f)),
            pl.BlockSpec((tf, D), lambda i, f: (f, 0)),
        ],
        out_specs=pl.BlockSpec((tm, D), lambda i, f: (i, 0)),
        out_shape=jax.ShapeDtypeStruct((T, D), F32),
        scratch_shapes=[pltpu.VMEM((tm, D), BF16)],
        compiler_params=_params(("parallel", "arbitrary")),
        name="ffn",
    )(x, g, wg, wu, wd)


def _rope_table_kernel(pos_ref, inv_ref, sign_ref, cos_ref, sin_ref):
    ang = pos_ref[...].astype(F32) * inv_ref[...]
    cos_ref[...] = jnp.cos(ang)
    sin_ref[...] = jnp.sin(ang) * sign_ref[...]


def _rope_tables(pos_col, inv_full, sign_full, *, tm=2048):
    T = pos_col.shape[0]
    return pl.pallas_call(
        _rope_table_kernel,
        grid=(T // tm,),
        in_specs=[
            pl.BlockSpec((tm, 1), lambda i: (i, 0)),
            pl.BlockSpec((1, LANES), lambda i: (0, 0)),
            pl.BlockSpec((1, LANES), lambda i: (0, 0)),
        ],
        out_specs=[pl.BlockSpec((tm, LANES), lambda i: (i, 0))] * 2,
        out_shape=[jax.ShapeDtypeStruct((T, LANES), F32)] * 2,
        compiler_params=_params(("parallel",)),
        name="rope_tables",
    )(pos_col, inv_full, sign_full)


def _inproj_kernel(x_ref, gmix_ref, win_ref, gqa_ref, gka_ref, gcq_ref, wuq_ref, gckv_ref, wukv_ref,
                   gqn_ref, gqr_ref, gkn_ref, gkr_ref, cos_ref, sin_ref,
                   qa_ref, ka_ref, va_ref, qb_ref, kb_ref, vb_ref):
    n = _rms(x_ref[...], gmix_ref[...]).astype(BF16)
    z = jnp.dot(n, win_ref[...], preferred_element_type=F32)

    low = lax.broadcasted_iota(jnp.int32, (1, LANES), 1) < SWA_HEAD_DIM

    def head_pair_norm(blk, g):
        sq = blk * blk
        s_lo = jnp.sum(jnp.where(low, sq, 0.0), axis=-1, keepdims=True)
        s_hi = jnp.sum(jnp.where(low, 0.0, sq), axis=-1, keepdims=True)
        r = jnp.where(low, lax.rsqrt(s_lo / SWA_HEAD_DIM + EPS), lax.rsqrt(s_hi / SWA_HEAD_DIM + EPS))
        return blk * r * g

    nq = SWA_Q_HEADS * SWA_HEAD_DIM
    nkv = SWA_KV_HEADS * SWA_HEAD_DIM
    for c in range(nq // LANES):
        qa_ref[:, c * LANES:(c + 1) * LANES] = head_pair_norm(
            z[:, c * LANES:(c + 1) * LANES], gqa_ref[...]).astype(BF16)
    for c in range(nkv // LANES):
        ka_ref[:, c * LANES:(c + 1) * LANES] = head_pair_norm(
            z[:, nq + c * LANES:nq + (c + 1) * LANES], gka_ref[...]).astype(BF16)
    va_ref[...] = z[:, nq + nkv:nq + 2 * nkv].astype(BF16)

    o_cq = nq + 2 * nkv
    o_ckv = o_cq + MLA_Q_RANK
    o_kr = o_ckv + MLA_KV_RANK
    cq = _rms(z[:, o_cq:o_ckv], gcq_ref[...]).astype(BF16)
    qb = jnp.dot(cq, wuq_ref[...], preferred_element_type=F32)
    ckv = _rms(z[:, o_ckv:o_kr], gckv_ref[...]).astype(BF16)
    kv = jnp.dot(ckv, wukv_ref[...], preferred_element_type=F32)
    kr = z[:, o_kr:o_kr + LANES]

    cos = cos_ref[...]
    sin = sin_ref[...]

    def rope(t):
        return t * cos + pltpu.roll(t, LANES // 2, 1) * sin

    kr_rot = rope(kr * gkr_ref[...])
    ssq_kr = jnp.sum(kr * kr, axis=-1, keepdims=True)
    for h in range(MLA_HEADS):
        qn = qb[:, h * LANES:(h + 1) * LANES]
        qr = qb[:, (MLA_HEADS + h) * LANES:(MLA_HEADS + h + 1) * LANES]
        ssq = jnp.sum(qn * qn, axis=-1, keepdims=True) + jnp.sum(qr * qr, axis=-1, keepdims=True)
        r = lax.rsqrt(ssq / MLA_QK + EPS)
        qb_ref[0, h, :, 0:LANES] = (qn * r * gqn_ref[...]).astype(BF16)
        qb_ref[0, h, :, LANES:2 * LANES] = rope(qr * r * gqr_ref[...]).astype(BF16)
        kn = kv[:, 2 * h * LANES:(2 * h + 1) * LANES]
        vh = kv[:, (2 * h + 1) * LANES:(2 * h + 2) * LANES]
        rk = lax.rsqrt((jnp.sum(kn * kn, axis=-1, keepdims=True) + ssq_kr) / MLA_QK + EPS)
        kb_ref[0, h, :, 0:LANES] = (kn * rk * gkn_ref[...]).astype(BF16)
        kb_ref[0, h, :, LANES:2 * LANES] = (kr_rot * rk).astype(BF16)
        vb_ref[0, h] = vh.astype(BF16)


def _inproj(x, gmix, win, gqa, gka, gcq, wuq, gckv, wukv, gqn, gqr, gkn, gkr, cos, sin, *, B, S, tm=256):
    T, D = x.shape
    ns = S // tm
    nq = SWA_Q_HEADS * SWA_HEAD_DIM
    nkv = SWA_KV_HEADS * SWA_HEAD_DIM

    def full(a):
        return pl.BlockSpec(a.shape, lambda i: (0,) * a.ndim)

    row = lambda w: pl.BlockSpec((tm, w), lambda i: (i, 0))
    head = lambda w: pl.BlockSpec((1, MLA_HEADS, tm, w), lambda i: (i // ns, 0, i % ns, 0))
    return pl.pallas_call(
        _inproj_kernel,
        grid=(T // tm,),
        in_specs=[row(D), full(gmix), full(win), full(gqa), full(gka), full(gcq), full(wuq), full(gckv),
                  full(wukv), full(gqn), full(gqr), full(gkn), full(gkr), row(LANES), row(LANES)],
        out_specs=[row(nq), row(nkv), row(nkv), head(MLA_QK_PAD), head(MLA_QK_PAD), head(MLA_V)],
        out_shape=[
            jax.ShapeDtypeStruct((T, nq), BF16),
            jax.ShapeDtypeStruct((T, nkv), BF16),
            jax.ShapeDtypeStruct((T, nkv), BF16),
            jax.ShapeDtypeStruct((B, MLA_HEADS, S, MLA_QK_PAD), BF16),
            jax.ShapeDtypeStruct((B, MLA_HEADS, S, MLA_QK_PAD), BF16),
            jax.ShapeDtypeStruct((B, MLA_HEADS, S, MLA_V), BF16),
        ],
        compiler_params=_params(("parallel",)),
        name="inproj",
    )(x, gmix, win, gqa, gka, gcq, wuq, gckv, wukv, gqn, gqr, gkn, gkr, cos, sin)


def _swa_kernel(sink_ref, q_ref, kc_ref, kp_ref, vc_ref, vp_ref, pq_ref, pkc_ref, pkp_ref, o_ref):
    n = pl.program_id(1)
    blk = SWA_BLOCK
    pq = pq_ref[...]
    pk = jnp.concatenate([pkp_ref[0], pkc_ref[0]], axis=1)
    dist = jnp.abs(pq - pk).astype(F32)
    qi = lax.broadcasted_iota(jnp.int32, (blk, 2 * blk), 0)
    kj = lax.broadcasted_iota(jnp.int32, (blk, 2 * blk), 1)
    diff = qi + blk - kj
    valid = (diff >= 0) & (diff < blk) & ((n > 0) | (kj >= blk))
    low = lax.broadcasted_iota(jnp.int32, (1, LANES), 1) < SWA_HEAD_DIM
    scale = SWA_HEAD_DIM ** -0.5

    for t in range(SWA_KV_HEADS // 2):
        cols = slice(t * LANES, (t + 1) * LANES)
        k2 = jnp.concatenate([kp_ref[:, cols], kc_ref[:, cols]], axis=0)
        v2 = jnp.concatenate([vp_ref[:, cols], vc_ref[:, cols]], axis=0)
        qs = jnp.concatenate(
            [q_ref[:, (4 * t + i) * LANES:(4 * t + i + 1) * LANES] for i in range(SWA_GROUP)], axis=0)
        outs = []
        for half in range(2):
            g = 2 * t + half
            keep = low if half == 0 else jnp.logical_not(low)
            qm = jnp.where(keep, qs, jnp.zeros_like(qs))
            s_all = lax.dot_general(qm, k2, NT_DIMS, preferred_element_type=F32)
            probs = []
            for i in range(SWA_GROUP):
                hq = SWA_GROUP * g + i
                slope = float(2.0 ** (-8.0 * (hq + 1) / SWA_Q_HEADS))
                s = s_all[i * blk:(i + 1) * blk] * scale - slope * dist
                s = jnp.where(valid, s, -jnp.inf)
                sink = sink_ref[hq]
                m = jnp.maximum(jnp.max(s, axis=-1, keepdims=True), sink)
                e = jnp.exp(s - m)
                den = jnp.sum(e, axis=-1, keepdims=True) + jnp.exp(sink - m)
                probs.append((e / den).astype(BF16))
            p = jnp.concatenate(probs, axis=0)
            outs.append(jnp.dot(p, v2, preferred_element_type=F32))
        merged = jnp.where(low, outs[0], outs[1])
        for i in range(SWA_GROUP):
            o_ref[:, (4 * t + i) * LANES:(4 * t + i + 1) * LANES] = merged[i * blk:(i + 1) * blk]


def _swa(sinks, qa, ka, va, pos_col, pos_row, *, B, S):
    T = qa.shape[0]
    blk = SWA_BLOCK
    nb = S // blk
    nq = SWA_Q_HEADS * SWA_HEAD_DIM
    nkv = SWA_KV_HEADS * SWA_HEAD_DIM
    cur = lambda b, n: (b * nb + n, 0)
    prev = lambda b, n: (b * nb + jnp.maximum(n - 1, 0), 0)
    cur3 = lambda b, n: (b * nb + n, 0, 0)
    prev3 = lambda b, n: (b * nb + jnp.maximum(n - 1, 0), 0, 0)
    return pl.pallas_call(
        _swa_kernel,
        grid=(B, nb),
        in_specs=[
            pl.BlockSpec(memory_space=pltpu.SMEM),
            pl.BlockSpec((blk, nq), cur),
            pl.BlockSpec((blk, nkv), cur),
            pl.BlockSpec((blk, nkv), prev),
            pl.BlockSpec((blk, nkv), cur),
            pl.BlockSpec((blk, nkv), prev),
            pl.BlockSpec((blk, 1), cur),
            pl.BlockSpec((1, 1, blk), cur3),
            pl.BlockSpec((1, 1, blk), prev3),
        ],
        out_specs=pl.BlockSpec((blk, nq), cur),
        out_shape=jax.ShapeDtypeStruct((T, nq), F32),
        compiler_params=_params(("parallel", "arbitrary")),
        name="swa_attn",
    )(sinks, qa, ka, ka, va, va, pos_col, pos_row, pos_row)


def _mla_kernel(q_ref, k_ref, v_ref, o_ref, m_ref, l_ref, acc_ref, *, tq, tk):
    i = pl.program_id(2)
    q = q_ref[0, 0]
    scale = MLA_QK ** -0.5
    m_ref[...] = jnp.full(m_ref.shape, -jnp.inf, F32)
    l_ref[...] = jnp.zeros(l_ref.shape, F32)
    acc_ref[...] = jnp.zeros(acc_ref.shape, F32)

    def step(j, masked):
        off = pl.multiple_of(j * tk, tk)
        k = k_ref[0, 0, pl.ds(off, tk), :]
        v = v_ref[0, 0, pl.ds(off, tk), :]
        s = lax.dot_general(q, k, NT_DIMS, preferred_element_type=F32) * scale
        if masked:
            row = lax.broadcasted_iota(jnp.int32, (tq, tk), 0)
            col = lax.broadcasted_iota(jnp.int32, (tq, tk), 1)
            s = jnp.where(col <= row, s, -jnp.inf)
        m_prev = m_ref[...]
        m_new = jnp.maximum(m_prev, jnp.max(s, axis=-1, keepdims=True))
        alpha = jnp.exp(m_prev - m_new)
        p = jnp.exp(s - m_new)
        l_ref[...] = alpha * l_ref[...] + jnp.sum(p, axis=-1, keepdims=True)
        acc_ref[...] = alpha * acc_ref[...] + jnp.dot(p.astype(BF16), v, preferred_element_type=F32)
        m_ref[...] = m_new

    def body(j, c):
        step(j, False)
        return c

    lax.fori_loop(0, i, body, 0)
    step(i, True)
    o_ref[0] = acc_ref[...] / l_ref[...]


def _mla(qb, kb, vb, *, tq=512):
    B, H, S, _ = qb.shape
    tk = tq
    return pl.pallas_call(
        functools.partial(_mla_kernel, tq=tq, tk=tk),
        grid=(B, H, S // tq),
        in_specs=[
            pl.BlockSpec((1, 1, tq, MLA_QK_PAD), lambda b, h, i: (b, h, i, 0)),
            pl.BlockSpec((1, 1, S, MLA_QK_PAD), lambda b, h, i: (b, h, 0, 0)),
            pl.BlockSpec((1, 1, S, MLA_V), lambda b, h, i: (b, h, 0, 0)),
        ],
        out_specs=pl.BlockSpec((1, tq, MLA_V), lambda b, h, i: (b, i, h)),
        out_shape=jax.ShapeDtypeStruct((B, S, H * MLA_V), F32),
        scratch_shapes=[pltpu.VMEM((tq, 1), F32), pltpu.VMEM((tq, 1), F32), pltpu.VMEM((tq, MLA_V), F32)],
        compiler_params=_params(("parallel", "parallel", "arbitrary")),
        name="mla_attn",
    )(qb, kb, vb)


def _outproj_kernel(h_ref, oa_ref, ob_ref, ga_ref, gb_ref, woa_ref, wob_ref, o_ref):
    a = _rms(oa_ref[...], ga_ref[...]).astype(BF16)
    b = _rms(ob_ref[...], gb_ref[...]).astype(BF16)
    o_ref[...] = (h_ref[...] + jnp.dot(a, woa_ref[...], preferred_element_type=F32)
                  + jnp.dot(b, wob_ref[...], preferred_element_type=F32))


def _outproj(h, oa, ob, ga, gb, woa, wob, *, tm=512):
    T, D = h.shape
    row = lambda w: pl.BlockSpec((tm, w), lambda i: (i, 0))
    full = lambda a: pl.BlockSpec(a.shape, lambda i: (0,) * a.ndim)
    return pl.pallas_call(
        _outproj_kernel,
        grid=(T // tm,),
        in_specs=[row(D), row(oa.shape[1]), row(ob.shape[1]), full(ga), full(gb), full(woa), full(wob)],
        out_specs=row(D),
        out_shape=jax.ShapeDtypeStruct((T, D), F32),
        compiler_params=_params(("parallel",)),
        name="outproj",
    )(h, oa, ob, ga, gb, woa, wob)


def _ple_kernel(h_ref, p_ref, gg_ref, wgate_ref, wp_ref, gp_ref, o_ref):
    h = h_ref[...]
    gl = jnp.dot(_rms(h, gg_ref[...]).astype(BF16), wgate_ref[...], preferred_element_type=F32)
    gate = 1.0 / (1.0 + jnp.exp(-gl))
    pe = jnp.dot(p_ref[...].astype(BF16), wp_ref[...], preferred_element_type=F32)
    o_ref[...] = h + _rms(pe, gp_ref[...]) * gate


def _ple(h, p, gg, wgate, wp, gp, *, tm=512):
    T, D = h.shape
    row = lambda w: pl.BlockSpec((tm, w), lambda i: (i, 0))
    full = lambda a: pl.BlockSpec(a.shape, lambda i: (0,) * a.ndim)
    return pl.pallas_call(
        _ple_kernel,
        grid=(T // tm,),
        in_specs=[row(D), row(p.shape[1]), full(gg), full(wgate), full(wp), full(gp)],
        out_specs=row(D),
        out_shape=jax.ShapeDtypeStruct((T, D), F32),
        compiler_params=_params(("parallel",)),
        name="ple",
    )(h, p, gg, wgate, wp, gp)


def _swa_q_perm():
    idx = np.zeros(SWA_Q_HEADS * SWA_HEAD_DIM, np.int32)
    for t in range(SWA_KV_HEADS // 2):
        for i in range(SWA_GROUP):
            for half in range(2):
                head = 8 * t + 4 * half + i
                new0 = LANES * (4 * t + i) + SWA_HEAD_DIM * half
                idx[new0:new0 + SWA_HEAD_DIM] = SWA_HEAD_DIM * head + np.arange(SWA_HEAD_DIM)
    return idx


def _rope_pad_layout():
    idx = np.zeros(LANES, np.int32)
    ok = np.zeros(LANES, bool)
    half = MLA_ROPE // 2
    idx[0:half] = np.arange(half)
    ok[0:half] = True
    idx[LANES // 2:LANES // 2 + half] = half + np.arange(half)
    ok[LANES // 2:LANES // 2 + half] = True
    return idx, ok


def _take_cols(w, idx, ok=None):
    out = jnp.take(w, jnp.asarray(idx), axis=-1)
    if ok is not None:
        out = jnp.where(jnp.asarray(ok), out, jnp.zeros((), out.dtype))
    return out


def kernel(x, p, positions, ffn1_norm, ffn1_w_gate, ffn1_w_up, ffn1_w_down, mix_norm, w_in, swa_q_norm, swa_k_norm, swa_sinks, mla_q_lora_norm, mla_w_uq, mla_kv_lora_norm, mla_w_ukv, mla_q_norm, mla_k_norm, out_norm_swa, out_norm_mla, w_o, ffn2_norm, ffn2_w_gate, ffn2_w_up, ffn2_w_down, ple_proj, ple_proj_norm, ple_gate_norm, ple_gate):
    B, S, D = x.shape
    depth = p.shape[0]
    T = B * S
    nq = SWA_Q_HEADS * SWA_HEAD_DIM
    nkv = SWA_KV_HEADS * SWA_HEAD_DIM

    qperm = _swa_q_perm()
    rope_idx, rope_ok = _rope_pad_layout()
    o_cq = nq + 2 * nkv
    o_kr = o_cq + MLA_Q_RANK + MLA_KV_RANK
    win_idx = np.concatenate([qperm, np.arange(nq, o_kr), o_kr + rope_idx])
    win_ok = np.concatenate([np.ones(o_kr, bool), rope_ok])
    uq_idx = np.concatenate(
        [MLA_QK * h + np.arange(MLA_NOPE) for h in range(MLA_HEADS)]
        + [MLA_QK * h + MLA_NOPE + rope_idx for h in range(MLA_HEADS)])
    uq_ok = np.concatenate([np.ones(MLA_HEADS * MLA_NOPE, bool)] + [rope_ok] * MLA_HEADS)

    half = MLA_ROPE // 2
    inv_freq = ROPE_THETA ** (-jnp.arange(half, dtype=F32) / half)
    inv_full = _take_cols(jnp.concatenate([inv_freq, inv_freq]), rope_idx, rope_ok).reshape(1, LANES)
    sign_np = np.zeros(LANES, np.float32)
    sign_np[0:half] = -1.0
    sign_np[LANES // 2:LANES // 2 + half] = 1.0
    pos_col = positions.reshape(T, 1)
    pos_row = positions.reshape(T // SWA_BLOCK, 1, SWA_BLOCK)
    cos_t, sin_t = _rope_tables(pos_col, inv_full, jnp.asarray(sign_np).reshape(1, LANES))

    row = lambda v: v.reshape(1, -1)
    xf = x.reshape(T, D)
    for i in range(depth):
        w1g, w1u, w1d = ffn1_w_gate[i].astype(BF16), ffn1_w_up[i].astype(BF16), ffn1_w_down[i].astype(BF16)
        w2g, w2u, w2d = ffn2_w_gate[i].astype(BF16), ffn2_w_up[i].astype(BF16), ffn2_w_down[i].astype(BF16)
        win = _take_cols(w_in[i], win_idx, win_ok).astype(BF16)
        wuq = _take_cols(mla_w_uq[i], uq_idx, uq_ok).astype(BF16)
        wukv = mla_w_ukv[i].astype(BF16)
        woa = jnp.take(w_o[i][:nq], jnp.asarray(qperm), axis=0).astype(BF16)
        wob = w_o[i][nq:].astype(BF16)
        gqa = row(jnp.tile(swa_q_norm[i], 2))
        gka = row(jnp.tile(swa_k_norm[i], 2))
        gqn = row(mla_q_norm[i][:MLA_NOPE])
        gqr = row(_take_cols(mla_q_norm[i][MLA_NOPE:], rope_idx, rope_ok))
        gkn = row(mla_k_norm[i][:MLA_NOPE])
        gkr = row(_take_cols(mla_k_norm[i][MLA_NOPE:], rope_idx, rope_ok))
        ga = row(jnp.take(out_norm_swa[i], jnp.asarray(qperm)))

        h = _ffn(xf, row(ffn1_norm[i]), w1g, w1u, w1d)
        qa, ka, va, qb, kb, vb = _inproj(
            h, row(mix_norm[i]), win, gqa, gka, row(mla_q_lora_norm[i]), wuq, row(mla_kv_lora_norm[i]), wukv,
            gqn, gqr, gkn, gkr, cos_t, sin_t, B=B, S=S)
        oa = _swa(swa_sinks[i], qa, ka, va, pos_col, pos_row, B=B, S=S)
        ob = _mla(qb, kb, vb).reshape(T, MLA_HEADS * MLA_V)
        h = _outproj(h, oa, ob, ga, row(out_norm_mla[i]), woa, wob)
        h = _ffn(h, row(ffn2_norm[i]), w2g, w2u, w2d)
        xf = _ple(h, p[i].reshape(T, -1), row(ple_gate_norm[i]), ple_gate[i].astype(BF16),
                  ple_proj[i].astype(BF16), row(ple_proj_norm[i]))
    return xf.reshape(B, S, D)
```

```python
import functools
import math

import numpy as np
import jax
import jax.numpy as jnp
from jax import lax
from jax.experimental import pallas as pl
from jax.experimental.pallas import tpu as pltpu

F32 = jnp.float32
BF16 = jnp.bfloat16

EPS = 1e-6
ROPE_THETA = 10000.0
LANES = 128
SWA_BLOCK = 128
SWA_HEAD_DIM = 64
SWA_Q_HEADS = 16
SWA_KV_HEADS = 4
SWA_GROUP = 4
MLA_HEADS = 8
MLA_NOPE = 128
MLA_ROPE = 64
MLA_QK = MLA_NOPE + MLA_ROPE
MLA_V = 128
MLA_Q_RANK = 512
MLA_KV_RANK = 256
MLA_QK_PAD = 2 * LANES
MLA_Q_PRESCALE = (MLA_QK ** -0.5) * math.log2(math.e)
VMEM_LIMIT = 56 * 1024 * 1024

NT_DIMS = (((1,), (1,)), ((), ()))


def _rms(x, g):
    ms = jnp.mean(x * x, axis=-1, keepdims=True)
    return x * lax.rsqrt(ms + EPS) * g


def _params(sem, vmem=VMEM_LIMIT):
    return pltpu.CompilerParams(dimension_semantics=sem, vmem_limit_bytes=vmem)


def _ffn_kernel(x_ref, g_ref, wg_ref, wu_ref, wd_ref, o_ref, xn_ref):
    f = pl.program_id(1)
    last = pl.num_programs(1) - 1

    @pl.when(f == 0)
    def _():
        xn_ref[...] = _rms(x_ref[...], g_ref[...]).astype(BF16)

    xn = xn_ref[...]
    gate = jnp.dot(xn, wg_ref[...], preferred_element_type=F32)
    up = jnp.dot(xn, wu_ref[...], preferred_element_type=F32)
    h = (gate / (1.0 + jnp.exp(-gate)) * up).astype(BF16)
    contrib = jnp.dot(h, wd_ref[...], preferred_element_type=F32)

    @pl.when(f == 0)
    def _():
        o_ref[...] = contrib

    @pl.when(jnp.logical_and(f > 0, f < last))
    def _():
        o_ref[...] += contrib

    @pl.when(f == last)
    def _():
        o_ref[...] = x_ref[...] + 0.5 * (o_ref[...] + contrib)


def _ffn(x, g, wg, wu, wd, *, tm=1024, tf=512):
    T, D = x.shape
    Fd = wg.shape[1]
    assert T % tm == 0 and Fd % tf == 0
    return pl.pallas_call(
        _ffn_kernel,
        grid=(T // tm, Fd // tf),
        in_specs=[
            pl.BlockSpec((tm, D), lambda i, f: (i, 0), pipeline_mode=pl.Buffered(1)),
            pl.BlockSpec((1, D), lambda i, f: (0, 0)),
            pl.BlockSpec((D, tf), lambda i, f: (0, f)),
            pl.BlockSpec((D, tf), lambda i, f: (0, f)),
            pl.BlockSpec((tf, D), lambda i, f: (f, 0)),
        ],
        out_specs=pl.BlockSpec((tm, D), lambda i, f: (i, 0)),
        out_shape=jax.ShapeDtypeStruct((T, D), F32),
        scratch_shapes=[pltpu.VMEM((tm, D), BF16)],
        compiler_params=_params(("parallel", "arbitrary")),
        name="ffn",
    )(x, g, wg, wu, wd)


def _rope_table_kernel(pos_ref, inv_ref, sign_ref, cos_ref, sin_ref):
    ang = pos_ref[...].astype(F32) * inv_ref[...]
    cos_ref[...] = jnp.cos(ang)
    sin_ref[...] = jnp.sin(ang) * sign_ref[...]


def _rope_tables(pos_col, inv_full, sign_full, *, tm=2048):
    T = pos_col.shape[0]
    return pl.pallas_call(
        _rope_table_kernel,
        grid=(T // tm,),
        in_specs=[
            pl.BlockSpec((tm, 1), lambda i: (i, 0)),
            pl.BlockSpec((1, LANES), lambda i: (0, 0)),
            pl.BlockSpec((1, LANES), lambda i: (0, 0)),
        ],
        out_specs=[pl.BlockSpec((tm, LANES), lambda i: (i, 0))] * 2,
        out_shape=[jax.ShapeDtypeStruct((T, LANES), F32)] * 2,
        compiler_params=_params(("parallel",)),
        name="rope_tables",
    )(pos_col, inv_full, sign_full)


def _inproj_kernel(x_ref, gmix_ref, win_ref, gqa_ref, gka_ref, gcq_ref, wuq_ref, gckv_ref, wukv_ref,
                   gqn_ref, gqr_ref, gkn_ref, gkr_ref, cos_ref, sin_ref,
                   qa_ref, ka_ref, va_ref, qb_ref, kb_ref, vb_ref):
    n = _rms(x_ref[...], gmix_ref[...]).astype(BF16)
    z = jnp.dot(n, win_ref[...], preferred_element_type=F32)

    low = lax.broadcasted_iota(jnp.int32, (1, LANES), 1) < SWA_HEAD_DIM

    def head_pair_norm(blk, g):
        sq = blk * blk
        s_lo = jnp.sum(jnp.where(low, sq, 0.0), axis=-1, keepdims=True)
        s_hi = jnp.sum(jnp.where(low, 0.0, sq), axis=-1, keepdims=True)
        r = jnp.where(low, lax.rsqrt(s_lo / SWA_HEAD_DIM + EPS), lax.rsqrt(s_hi / SWA_HEAD_DIM + EPS))
        return blk * r * g

    nq = SWA_Q_HEADS * SWA_HEAD_DIM
    nkv = SWA_KV_HEADS * SWA_HEAD_DIM
    for c in range(nq // LANES):
        qa_ref[:, c * LANES:(c + 1) * LANES] = head_pair_norm(
            z[:, c * LANES:(c + 1) * LANES], gqa_ref[...]).astype(BF16)
    for c in range(nkv // LANES):
        ka_ref[:, c * LANES:(c + 1) * LANES] = head_pair_norm(
            z[:, nq + c * LANES:nq + (c + 1) * LANES], gka_ref[...]).astype(BF16)
    va_ref[...] = z[:, nq + nkv:nq + 2 * nkv].astype(BF16)

    o_cq = nq + 2 * nkv
    o_ckv = o_cq + MLA_Q_RANK
    o_kr = o_ckv + MLA_KV_RANK
    cq = _rms(z[:, o_cq:o_ckv], gcq_ref[...]).astype(BF16)
    qb = jnp.dot(cq, wuq_ref[...], preferred_element_type=F32)
    ckv = _rms(z[:, o_ckv:o_kr], gckv_ref[...]).astype(BF16)
    kv = jnp.dot(ckv, wukv_ref[...], preferred_element_type=F32)
    kr = z[:, o_kr:o_kr + LANES]

    cos = cos_ref[...]
    sin = sin_ref[...]

    def rope(t):
        return t * cos + pltpu.roll(t, LANES // 2, 1) * sin

    kr_rot = rope(kr * gkr_ref[...])
    ssq_kr = jnp.sum(kr * kr, axis=-1, keepdims=True)
    for h in range(MLA_HEADS):
        qn = qb[:, h * LANES:(h + 1) * LANES]
        qr = qb[:, (MLA_HEADS + h) * LANES:(MLA_HEADS + h + 1) * LANES]
        ssq = jnp.sum(qn * qn, axis=-1, keepdims=True) + jnp.sum(qr * qr, axis=-1, keepdims=True)
        r = lax.rsqrt(ssq / MLA_QK + EPS)
        qb_ref[0, h, :, 0:LANES] = (qn * r * gqn_ref[...] * MLA_Q_PRESCALE).astype(BF16)
        qb_ref[0, h, :, LANES:2 * LANES] = (rope(qr * r * gqr_ref[...]) * MLA_Q_PRESCALE).astype(BF16)
        kn = kv[:, 2 * h * LANES:(2 * h + 1) * LANES]
        vh = kv[:, (2 * h + 1) * LANES:(2 * h + 2) * LANES]
        rk = lax.rsqrt((jnp.sum(kn * kn, axis=-1, keepdims=True) + ssq_kr) / MLA_QK + EPS)
        kb_ref[0, h, :, 0:LANES] = (kn * rk * gkn_ref[...]).astype(BF16)
        kb_ref[0, h, :, LANES:2 * LANES] = (kr_rot * rk).astype(BF16)
        vb_ref[0, h] = vh.astype(BF16)


def _inproj(x, gmix, win, gqa, gka, gcq, wuq, gckv, wukv, gqn, gqr, gkn, gkr, cos, sin, *, B, S, tm=256):
    T, D = x.shape
    ns = S // tm
    nq = SWA_Q_HEADS * SWA_HEAD_DIM
    nkv = SWA_KV_HEADS * SWA_HEAD_DIM

    def full(a):
        return pl.BlockSpec(a.shape, lambda i: (0,) * a.ndim)

    row = lambda w: pl.BlockSpec((tm, w), lambda i: (i, 0))
    head = lambda w: pl.BlockSpec((1, MLA_HEADS, tm, w), lambda i: (i // ns, 0, i % ns, 0))
    return pl.pallas_call(
        _inproj_kernel,
        grid=(T // tm,),
        in_specs=[row(D), full(gmix), full(win), full(gqa), full(gka), full(gcq), full(wuq), full(gckv),
                  full(wukv), full(gqn), full(gqr), full(gkn), full(gkr), row(LANES), row(LANES)],
        out_specs=[row(nq), row(nkv), row(nkv), head(MLA_QK_PAD), head(MLA_QK_PAD), head(MLA_V)],
        out_shape=[
            jax.ShapeDtypeStruct((T, nq), BF16),
            jax.ShapeDtypeStruct((T, nkv), BF16),
            jax.ShapeDtypeStruct((T, nkv), BF16),
            jax.ShapeDtypeStruct((B, MLA_HEADS, S, MLA_QK_PAD), BF16),
            jax.ShapeDtypeStruct((B, MLA_HEADS, S, MLA_QK_PAD), BF16),
            jax.ShapeDtypeStruct((B, MLA_HEADS, S, MLA_V), BF16),
        ],
        compiler_params=_params(("parallel",)),
        name="inproj",
    )(x, gmix, win, gqa, gka, gcq, wuq, gckv, wukv, gqn, gqr, gkn, gkr, cos, sin)


def _swa_kernel(sink_ref, q_ref, kc_ref, kp_ref, vc_ref, vp_ref, pq_ref, pkc_ref, pkp_ref, o_ref):
    n = pl.program_id(1)
    blk = SWA_BLOCK
    pq = pq_ref[...]
    pk = jnp.concatenate([pkp_ref[0], pkc_ref[0]], axis=1)
    dist = jnp.abs(pq - pk).astype(F32)
    qi = lax.broadcasted_iota(jnp.int32, (blk, 2 * blk), 0)
    kj = lax.broadcasted_iota(jnp.int32, (blk, 2 * blk), 1)
    diff = qi + blk - kj
    valid = (diff >= 0) & (diff < blk) & ((n > 0) | (kj >= blk))
    low = lax.broadcasted_iota(jnp.int32, (1, LANES), 1) < SWA_HEAD_DIM
    scale = SWA_HEAD_DIM ** -0.5

    for t in range(SWA_KV_HEADS // 2):
        cols = slice(t * LANES, (t + 1) * LANES)
        k2 = jnp.concatenate([kp_ref[:, cols], kc_ref[:, cols]], axis=0)
        v2 = jnp.concatenate([vp_ref[:, cols], vc_ref[:, cols]], axis=0)
        qs = jnp.concatenate(
            [q_ref[:, (4 * t + i) * LANES:(4 * t + i + 1) * LANES] for i in range(SWA_GROUP)], axis=0)
        outs = []
        for half in range(2):
            g = 2 * t + half
            keep = low if half == 0 else jnp.logical_not(low)
            qm = jnp.where(keep, qs, jnp.zeros_like(qs))
            s_all = lax.dot_general(qm, k2, NT_DIMS, preferred_element_type=F32)
            probs = []
            for i in range(SWA_GROUP):
                hq = SWA_GROUP * g + i
                slope = float(2.0 ** (-8.0 * (hq + 1) / SWA_Q_HEADS))
                s = s_all[i * blk:(i + 1) * blk] * scale - slope * dist
                s = jnp.where(valid, s, -jnp.inf)
                sink = sink_ref[hq]
                m = jnp.maximum(jnp.max(s, axis=-1, keepdims=True), sink)
                e = jnp.exp(s - m)
                den = jnp.sum(e, axis=-1, keepdims=True) + jnp.exp(sink - m)
                probs.append((e / den).astype(BF16))
            p = jnp.concatenate(probs, axis=0)
            outs.append(jnp.dot(p, v2, preferred_element_type=F32))
        merged = jnp.where(low, outs[0], outs[1])
        for i in range(SWA_GROUP):
            o_ref[:, (4 * t + i) * LANES:(4 * t + i + 1) * LANES] = merged[i * blk:(i + 1) * blk]


def _swa(sinks, qa, ka, va, pos_col, pos_row, *, B, S):
    T = qa.shape[0]
    blk = SWA_BLOCK
    nb = S // blk
    nq = SWA_Q_HEADS * SWA_HEAD_DIM
    nkv = SWA_KV_HEADS * SWA_HEAD_DIM
    cur = lambda b, n: (b * nb + n, 0)
    prev = lambda b, n: (b * nb + jnp.maximum(n - 1, 0), 0)
    cur3 = lambda b, n: (b * nb + n, 0, 0)
    prev3 = lambda b, n: (b * nb + jnp.maximum(n - 1, 0), 0, 0)
    return pl.pallas_call(
        _swa_kernel,
        grid=(B, nb),
        in_specs=[
            pl.BlockSpec(memory_space=pltpu.SMEM),
            pl.BlockSpec((blk, nq), cur),
            pl.BlockSpec((blk, nkv), cur),
            pl.BlockSpec((blk, nkv), prev),
            pl.BlockSpec((blk, nkv), cur),
            pl.BlockSpec((blk, nkv), prev),
            pl.BlockSpec((blk, 1), cur),
            pl.BlockSpec((1, 1, blk), cur3),
            pl.BlockSpec((1, 1, blk), prev3),
        ],
        out_specs=pl.BlockSpec((blk, nq), cur),
        out_shape=jax.ShapeDtypeStruct((T, nq), F32),
        compiler_params=_params(("parallel", "arbitrary")),
        name="swa_attn",
    )(sinks, qa, ka, ka, va, va, pos_col, pos_row, pos_row)


MLA_TILE = 1024
MLA_SUB = 256


def _softmax_chain(s_ref, rows, nkeys, diag, m_ref, l_ref, acc_ref):
    blocks = [s_ref[rows, b * LANES:(b + 1) * LANES] for b in range(nkeys // LANES)]
    if diag:
        row = lax.broadcasted_iota(jnp.int32, (MLA_SUB, LANES), 0)
        lane = lax.broadcasted_iota(jnp.int32, (MLA_SUB, LANES), 1)
        nd = MLA_SUB // LANES
        for d in range(nd):
            b = len(blocks) - nd + d
            blocks[b] = jnp.where(lane + d * LANES <= row, blocks[b], -jnp.inf)
    mx = functools.reduce(jnp.maximum, blocks)
    m_prev = m_ref[rows, :]
    m_new = jnp.maximum(m_prev, jnp.max(mx, axis=-1, keepdims=True))
    alpha = jnp.exp2(m_prev - m_new)
    ps = [jnp.exp2(b - m_new) for b in blocks]
    l_ref[rows, :] = alpha * l_ref[rows, :] + functools.reduce(jnp.add, ps)
    acc_ref[rows, :] = alpha * acc_ref[rows, :]
    m_ref[rows, :] = m_new
    return jnp.concatenate([pb.astype(BF16) for pb in ps], axis=-1)


def _mla_kernel(q_ref, k_ref, v_ref, o_ref, s0_ref, s1_ref, m_ref, l_ref, acc_ref):
    i = pl.program_id(2)
    nchain = MLA_TILE // MLA_SUB
    rows = [slice(c * MLA_SUB, (c + 1) * MLA_SUB) for c in range(nchain)]
    m_ref[...] = jnp.full(m_ref.shape, -jnp.inf, F32)
    l_ref[...] = jnp.zeros(l_ref.shape, F32)
    acc_ref[...] = jnp.zeros(acc_ref.shape, F32)

    def chunk_rows(chunk):
        return pl.ds(pl.multiple_of(chunk * MLA_TILE, MLA_TILE), MLA_TILE)

    def scores(chunk, s_ref):
        k = k_ref[0, 0, chunk_rows(chunk), :]
        s_ref[...] = lax.dot_general(q_ref[0, 0], k, NT_DIMS, preferred_element_type=F32)

    def full_chunk(chunk, s_ref):
        p = jnp.concatenate(
            [_softmax_chain(s_ref, rows[c], MLA_TILE, False, m_ref, l_ref, acc_ref) for c in range(nchain)], axis=0)
        acc_ref[...] += jnp.dot(p, v_ref[0, 0, chunk_rows(chunk), :], preferred_element_type=F32)

    def diag_chunk(s_ref):
        base = pl.multiple_of(i * MLA_TILE, MLA_TILE)
        for c in range(nchain):
            nkeys = (c + 1) * MLA_SUB
            p = _softmax_chain(s_ref, rows[c], nkeys, True, m_ref, l_ref, acc_ref)
            acc_ref[rows[c], :] += jnp.dot(p, v_ref[0, 0, pl.ds(base, nkeys), :], preferred_element_type=F32)

    scores(0, s0_ref)

    def pair(jj, carry):
        scores(2 * jj + 1, s1_ref)
        full_chunk(2 * jj, s0_ref)
        scores(2 * jj + 2, s0_ref)
        full_chunk(2 * jj + 1, s1_ref)
        return carry

    lax.fori_loop(0, i // 2, pair, 0)

    @pl.when(i % 2 == 0)
    def _():
        diag_chunk(s0_ref)

    @pl.when(i % 2 == 1)
    def _():
        scores(i, s1_ref)
        full_chunk(i - 1, s0_ref)
        diag_chunk(s1_ref)

    l = jnp.sum(l_ref[...], axis=-1, keepdims=True)
    o_ref[0] = acc_ref[...] / l


def _mla(qb, kb, vb):
    B, H, S, _ = qb.shape
    assert S % MLA_TILE == 0 and MLA_TILE % MLA_SUB == 0
    return pl.pallas_call(
        _mla_kernel,
        grid=(B, H, S // MLA_TILE),
        in_specs=[
            pl.BlockSpec((1, 1, MLA_TILE, MLA_QK_PAD), lambda b, h, i: (b, h, i, 0)),
            pl.BlockSpec((1, 1, S, MLA_QK_PAD), lambda b, h, i: (b, h, 0, 0)),
            pl.BlockSpec((1, 1, S, MLA_V), lambda b, h, i: (b, h, 0, 0)),
        ],
        out_specs=pl.BlockSpec((1, MLA_TILE, MLA_V), lambda b, h, i: (b, i, h)),
        out_shape=jax.ShapeDtypeStruct((B, S, H * MLA_V), F32),
        scratch_shapes=[pltpu.VMEM((MLA_TILE, MLA_TILE), F32), pltpu.VMEM((MLA_TILE, MLA_TILE), F32),
                        pltpu.VMEM((MLA_TILE, LANES), F32), pltpu.VMEM((MLA_TILE, LANES), F32),
                        pltpu.VMEM((MLA_TILE, MLA_V), F32)],
        compiler_params=_params(("parallel", "parallel", "arbitrary")),
        name="mla_attn",
    )(qb, kb, vb)


def _outproj_kernel(h_ref, oa_ref, ob_ref, ga_ref, gb_ref, woa_ref, wob_ref, o_ref):
    a = _rms(oa_ref[...], ga_ref[...]).astype(BF16)
    b = _rms(ob_ref[...], gb_ref[...]).astype(BF16)
    o_ref[...] = (h_ref[...] + jnp.dot(a, woa_ref[...], preferred_element_type=F32)
                  + jnp.dot(b, wob_ref[...], preferred_element_type=F32))


def _outproj(h, oa, ob, ga, gb, woa, wob, *, tm=512):
    T, D = h.shape
    row = lambda w: pl.BlockSpec((tm, w), lambda i: (i, 0))
    full = lambda a: pl.BlockSpec(a.shape, lambda i: (0,) * a.ndim)
    return pl.pallas_call(
        _outproj_kernel,
        grid=(T // tm,),
        in_specs=[row(D), row(oa.shape[1]), row(ob.shape[1]), full(ga), full(gb), full(woa), full(wob)],
        out_specs=row(D),
        out_shape=jax.ShapeDtypeStruct((T, D), F32),
        compiler_params=_params(("parallel",)),
        name="outproj",
    )(h, oa, ob, ga, gb, woa, wob)


def _ple_kernel(h_ref, p_ref, gg_ref, wgate_ref, wp_ref, gp_ref, o_ref):
    h = h_ref[...]
    gl = jnp.dot(_rms(h, gg_ref[...]).astype(BF16), wgate_ref[...], preferred_element_type=F32)
    gate = 1.0 / (1.0 + jnp.exp(-gl))
    pe = jnp.dot(p_ref[...].astype(BF16), wp_ref[...], preferred_element_type=F32)
    o_ref[...] = h + _rms(pe, gp_ref[...]) * gate


def _ple(h, p, gg, wgate, wp, gp, *, tm=512):
    T, D = h.shape
    row = lambda w: pl.BlockSpec((tm, w), lambda i: (i, 0))
    full = lambda a: pl.BlockSpec(a.shape, lambda i: (0,) * a.ndim)
    return pl.pallas_call(
        _ple_kernel,
        grid=(T // tm,),
        in_specs=[row(D), row(p.shape[1]), full(gg), full(wgate), full(wp), full(gp)],
        out_specs=row(D),
        out_shape=jax.ShapeDtypeStruct((T, D), F32),
        compiler_params=_params(("parallel",)),
        name="ple",
    )(h, p, gg, wgate, wp, gp)


def _swa_q_perm():
    idx = np.zeros(SWA_Q_HEADS * SWA_HEAD_DIM, np.int32)
    for t in range(SWA_KV_HEADS // 2):
        for i in range(SWA_GROUP):
            for half in range(2):
                head = 8 * t + 4 * half + i
                new0 = LANES * (4 * t + i) + SWA_HEAD_DIM * half
                idx[new0:new0 + SWA_HEAD_DIM] = SWA_HEAD_DIM * head + np.arange(SWA_HEAD_DIM)
    return idx


def _rope_pad_layout():
    idx = np.zeros(LANES, np.int32)
    ok = np.zeros(LANES, bool)
    half = MLA_ROPE // 2
    idx[0:half] = np.arange(half)
    ok[0:half] = True
    idx[LANES // 2:LANES // 2 + half] = half + np.arange(half)
    ok[LANES // 2:LANES // 2 + half] = True
    return idx, ok


def _take_cols(w, idx, ok=None):
    out = jnp.take(w, jnp.asarray(idx), axis=-1)
    if ok is not None:
        out = jnp.where(jnp.asarray(ok), out, jnp.zeros((), out.dtype))
    return out


def kernel(x, p, positions, ffn1_norm, ffn1_w_gate, ffn1_w_up, ffn1_w_down, mix_norm, w_in, swa_q_norm, swa_k_norm, swa_sinks, mla_q_lora_norm, mla_w_uq, mla_kv_lora_norm, mla_w_ukv, mla_q_norm, mla_k_norm, out_norm_swa, out_norm_mla, w_o, ffn2_norm, ffn2_w_gate, ffn2_w_up, ffn2_w_down, ple_proj, ple_proj_norm, ple_gate_norm, ple_gate):
    B, S, D = x.shape
    depth = p.shape[0]
    T = B * S
    nq = SWA_Q_HEADS * SWA_HEAD_DIM
    nkv = SWA_KV_HEADS * SWA_HEAD_DIM

    qperm = _swa_q_perm()
    rope_idx, rope_ok = _rope_pad_layout()
    o_cq = nq + 2 * nkv
    o_kr = o_cq + MLA_Q_RANK + MLA_KV_RANK
    win_idx = np.concatenate([qperm, np.arange(nq, o_kr), o_kr + rope_idx])
    win_ok = np.concatenate([np.ones(o_kr, bool), rope_ok])
    uq_idx = np.concatenate(
        [MLA_QK * h + np.arange(MLA_NOPE) for h in range(MLA_HEADS)]
        + [MLA_QK * h + MLA_NOPE + rope_idx for h in range(MLA_HEADS)])
    uq_ok = np.concatenate([np.ones(MLA_HEADS * MLA_NOPE, bool)] + [rope_ok] * MLA_HEADS)

    half = MLA_ROPE // 2
    inv_freq = ROPE_THETA ** (-jnp.arange(half, dtype=F32) / half)
    inv_full = _take_cols(jnp.concatenate([inv_freq, inv_freq]), rope_idx, rope_ok).reshape(1, LANES)
    sign_np = np.zeros(LANES, np.float32)
    sign_np[0:half] = -1.0
    sign_np[LANES // 2:LANES // 2 + half] = 1.0
    pos_col = positions.reshape(T, 1)
    pos_row = positions.reshape(T // SWA_BLOCK, 1, SWA_BLOCK)
    cos_t, sin_t = _rope_tables(pos_col, inv_full, jnp.asarray(sign_np).reshape(1, LANES))

    row = lambda v: v.reshape(1, -1)
    xf = x.reshape(T, D)
    for i in range(depth):
        w1g, w1u, w1d = ffn1_w_gate[i].astype(BF16), ffn1_w_up[i].astype(BF16), ffn1_w_down[i].astype(BF16)
        w2g, w2u, w2d = ffn2_w_gate[i].astype(BF16), ffn2_w_up[i].astype(BF16), ffn2_w_down[i].astype(BF16)
        win = _take_cols(w_in[i], win_idx, win_ok).astype(BF16)
        wuq = _take_cols(mla_w_uq[i], uq_idx, uq_ok).astype(BF16)
        wukv = mla_w_ukv[i].astype(BF16)
        woa = jnp.take(w_o[i][:nq], jnp.asarray(qperm), axis=0).astype(BF16)
        wob = w_o[i][nq:].astype(BF16)
        gqa = row(jnp.tile(swa_q_norm[i], 2))
        gka = row(jnp.tile(swa_k_norm[i], 2))
        gqn = row(mla_q_norm[i][:MLA_NOPE])
        gqr = row(_take_cols(mla_q_norm[i][MLA_NOPE:], rope_idx, rope_ok))
        gkn = row(mla_k_norm[i][:MLA_NOPE])
        gkr = row(_take_cols(mla_k_norm[i][MLA_NOPE:], rope_idx, rope_ok))
        ga = row(jnp.take(out_norm_swa[i], jnp.asarray(qperm)))

        h = _ffn(xf, row(ffn1_norm[i]), w1g, w1u, w1d)
        qa, ka, va, qb, kb, vb = _inproj(
            h, row(mix_norm[i]), win, gqa, gka, row(mla_q_lora_norm[i]), wuq, row(mla_kv_lora_norm[i]), wukv,
            gqn, gqr, gkn, gkr, cos_t, sin_t, B=B, S=S)
        oa = _swa(swa_sinks[i], qa, ka, va, pos_col, pos_row, B=B, S=S)
        ob = _mla(qb, kb, vb).reshape(T, MLA_HEADS * MLA_V)
        h = _outproj(h, oa, ob, ga, row(out_norm_mla[i]), woa, wob)
        h = _ffn(h, row(ffn2_norm[i]), w2g, w2u, w2d)
        xf = _ple(h, p[i].reshape(T, -1), row(ple_gate_norm[i]), ple_gate[i].astype(BF16),
                  ple_proj[i].astype(BF16), row(ple_proj_norm[i]))
    return xf.reshape(B, S, D)
```

```python
import functools
import math

import jax
import jax.numpy as jnp
from jax import lax
from jax.experimental import pallas as pl
from jax.experimental.pallas import tpu as pltpu

F32 = jnp.float32
BF16 = jnp.bfloat16

EPS = 1e-6
ROPE_THETA = 10000.0
LANES = 128
SWA_BLOCK = 128
SWA_HEAD_DIM = 64
SWA_Q_HEADS = 16
SWA_KV_HEADS = 4
SWA_GROUP = 4
MLA_HEADS = 8
MLA_NOPE = 128
MLA_ROPE = 64
MLA_QK = MLA_NOPE + MLA_ROPE
MLA_V = 128
MLA_Q_RANK = 512
MLA_KV_RANK = 256
MLA_QK_PAD = 2 * LANES
MLA_Q_PRESCALE = (MLA_QK ** -0.5) * math.log2(math.e)
VMEM_LIMIT = 56 * 1024 * 1024
FFN_VMEM_LIMIT = 60 * 1024 * 1024

NT_DIMS = (((1,), (1,)), ((), ()))


def _rms(x, g):
    ms = jnp.mean(x * x, axis=-1, keepdims=True)
    return x * lax.rsqrt(ms + EPS) * g


def _params(sem, vmem=VMEM_LIMIT):
    return pltpu.CompilerParams(dimension_semantics=sem, vmem_limit_bytes=vmem)


def _layer_spec(a, layer):
    nd = a.ndim - 1
    return pl.BlockSpec((None,) + a.shape[1:], lambda *_: (layer,) + (0,) * nd, pipeline_mode=pl.Buffered(1))


def _ffn_kernel(x_ref, g_ref, wg_ref, wu_ref, wd_ref, o_ref, xn_ref):
    @pl.when(pl.program_id(1) == 0)
    def _():
        x = x_ref[...]
        xn_ref[...] = _rms(x, g_ref[...]).astype(BF16)
        o_ref[...] = x

    xn = xn_ref[...]
    gate = jnp.dot(xn, wg_ref[...].astype(BF16), preferred_element_type=F32)
    up = jnp.dot(xn, wu_ref[...].astype(BF16), preferred_element_type=F32)
    h = (0.5 * gate / (1.0 + jnp.exp(-gate)) * up).astype(BF16)
    o_ref[...] += jnp.dot(h, wd_ref[...].astype(BF16), preferred_element_type=F32)


def _ffn(x, g, wg, wu, wd, layer, *, tm=1024, tf=512):
    T, D = x.shape
    Fd = wg.shape[2]
    assert T % tm == 0 and Fd % tf == 0
    return pl.pallas_call(
        _ffn_kernel,
        grid=(T // tm, Fd // tf),
        in_specs=[
            pl.BlockSpec((tm, D), lambda i, f: (i, 0), pipeline_mode=pl.Buffered(1)),
            _layer_spec(g, layer),
            pl.BlockSpec((None, D, tf), lambda i, f: (layer, 0, f)),
            pl.BlockSpec((None, D, tf), lambda i, f: (layer, 0, f)),
            pl.BlockSpec((None, tf, D), lambda i, f: (layer, f, 0)),
        ],
        out_specs=pl.BlockSpec((tm, D), lambda i, f: (i, 0)),
        out_shape=jax.ShapeDtypeStruct((T, D), F32),
        scratch_shapes=[pltpu.VMEM((tm, D), BF16)],
        compiler_params=_params(("parallel", "arbitrary"), FFN_VMEM_LIMIT),
        name="ffn",
    )(x, g, wg, wu, wd)


def _rope_table_kernel(pos_ref, inv_ref, sign_ref, cos_ref, sin_ref):
    ang = pos_ref[...].astype(F32) * inv_ref[...]
    cos_ref[...] = jnp.cos(ang)
    sin_ref[...] = jnp.sin(ang) * sign_ref[...]


def _rope_tables(pos_col, inv_full, sign_full, *, tm=2048):
    T = pos_col.shape[0]
    return pl.pallas_call(
        _rope_table_kernel,
        grid=(T // tm,),
        in_specs=[
            pl.BlockSpec((tm, 1), lambda i: (i, 0)),
            pl.BlockSpec((1, LANES), lambda i: (0, 0)),
            pl.BlockSpec((1, LANES), lambda i: (0, 0)),
        ],
        out_specs=[pl.BlockSpec((tm, LANES), lambda i: (i, 0))] * 2,
        out_shape=[jax.ShapeDtypeStruct((T, LANES), F32)] * 2,
        compiler_params=_params(("parallel",)),
        name="rope_tables",
    )(pos_col, inv_full, sign_full)


def _inproj_kernel(x_ref, gmix_ref, win_ref, gqa_ref, gka_ref, gcq_ref, wuq_ref, gckv_ref, wukv_ref,
                   gqn_ref, gqr_ref, gkn_ref, gkr_ref, cos_ref, sin_ref,
                   qa_ref, ka_ref, va_ref, qb_ref, kb_ref, vb_ref):
    n = _rms(x_ref[...], gmix_ref[...]).astype(BF16)
    z = jnp.dot(n, win_ref[...], preferred_element_type=F32)

    low = lax.broadcasted_iota(jnp.int32, (1, LANES), 1) < SWA_HEAD_DIM

    def head_pair_norm(blk, g):
        sq = blk * blk
        s_lo = jnp.sum(jnp.where(low, sq, 0.0), axis=-1, keepdims=True)
        s_hi = jnp.sum(jnp.where(low, 0.0, sq), axis=-1, keepdims=True)
        r = jnp.where(low, lax.rsqrt(s_lo / SWA_HEAD_DIM + EPS), lax.rsqrt(s_hi / SWA_HEAD_DIM + EPS))
        return blk * r * g

    nq = SWA_Q_HEADS * SWA_HEAD_DIM
    nkv = SWA_KV_HEADS * SWA_HEAD_DIM
    for c in range(nq // LANES):
        qa_ref[:, c * LANES:(c + 1) * LANES] = head_pair_norm(
            z[:, c * LANES:(c + 1) * LANES], gqa_ref[...]).astype(BF16)
    for c in range(nkv // LANES):
        ka_ref[:, c * LANES:(c + 1) * LANES] = head_pair_norm(
            z[:, nq + c * LANES:nq + (c + 1) * LANES], gka_ref[...]).astype(BF16)
    va_ref[...] = z[:, nq + nkv:nq + 2 * nkv].astype(BF16)

    o_cq = nq + 2 * nkv
    o_ckv = o_cq + MLA_Q_RANK
    o_kr = o_ckv + MLA_KV_RANK
    cq = _rms(z[:, o_cq:o_ckv], gcq_ref[...]).astype(BF16)
    qb = jnp.dot(cq, wuq_ref[...], preferred_element_type=F32)
    ckv = _rms(z[:, o_ckv:o_kr], gckv_ref[...]).astype(BF16)
    kv = jnp.dot(ckv, wukv_ref[...], preferred_element_type=F32)
    kr = z[:, o_kr:o_kr + LANES]

    cos = cos_ref[...]
    sin = sin_ref[...]

    def rope(t):
        return t * cos + pltpu.roll(t, LANES // 2, 1) * sin

    kr_rot = rope(kr * gkr_ref[...])
    ssq_kr = jnp.sum(kr * kr, axis=-1, keepdims=True)
    for h in range(MLA_HEADS):
        qn = qb[:, h * LANES:(h + 1) * LANES]
        qr = qb[:, (MLA_HEADS + h) * LANES:(MLA_HEADS + h + 1) * LANES]
        ssq = jnp.sum(qn * qn, axis=-1, keepdims=True) + jnp.sum(qr * qr, axis=-1, keepdims=True)
        r = lax.rsqrt(ssq / MLA_QK + EPS)
        qb_ref[0, h, :, 0:LANES] = (qn * r * gqn_ref[...] * MLA_Q_PRESCALE).astype(BF16)
        qb_ref[0, h, :, LANES:2 * LANES] = (rope(qr * r * gqr_ref[...]) * MLA_Q_PRESCALE).astype(BF16)
        kn = kv[:, 2 * h * LANES:(2 * h + 1) * LANES]
        vh = kv[:, (2 * h + 1) * LANES:(2 * h + 2) * LANES]
        rk = lax.rsqrt((jnp.sum(kn * kn, axis=-1, keepdims=True) + ssq_kr) / MLA_QK + EPS)
        kb_ref[0, h, :, 0:LANES] = (kn * rk * gkn_ref[...]).astype(BF16)
        kb_ref[0, h, :, LANES:2 * LANES] = (kr_rot * rk).astype(BF16)
        vb_ref[0, h] = vh.astype(BF16)


def _inproj(x, gmix, win, gqa, gka, gcq, wuq, gckv, wukv, gqn, gqr, gkn, gkr, cos, sin, layer, *, B, S, tm=512):
    T, D = x.shape
    ns = S // tm
    nq = SWA_Q_HEADS * SWA_HEAD_DIM
    nkv = SWA_KV_HEADS * SWA_HEAD_DIM
    full = lambda a: _layer_spec(a, layer)

    row = lambda w: pl.BlockSpec((tm, w), lambda i: (i, 0))
    head = lambda w: pl.BlockSpec((1, MLA_HEADS, tm, w), lambda i: (i // ns, 0, i % ns, 0))
    return pl.pallas_call(
        _inproj_kernel,
        grid=(T // tm,),
        in_specs=[row(D), full(gmix), full(win), full(gqa), full(gka), full(gcq), full(wuq), full(gckv),
                  full(wukv), full(gqn), full(gqr), full(gkn), full(gkr), row(LANES), row(LANES)],
        out_specs=[row(nq), row(nkv), row(nkv), head(MLA_QK_PAD), head(MLA_QK_PAD), head(MLA_V)],
        out_shape=[
            jax.ShapeDtypeStruct((T, nq), BF16),
            jax.ShapeDtypeStruct((T, nkv), BF16),
            jax.ShapeDtypeStruct((T, nkv), BF16),
            jax.ShapeDtypeStruct((B, MLA_HEADS, S, MLA_QK_PAD), BF16),
            jax.ShapeDtypeStruct((B, MLA_HEADS, S, MLA_QK_PAD), BF16),
            jax.ShapeDtypeStruct((B, MLA_HEADS, S, MLA_V), BF16),
        ],
        compiler_params=_params(("parallel",)),
        name="inproj",
    )(x, gmix, win, gqa, gka, gcq, wuq, gckv, wukv, gqn, gqr, gkn, gkr, cos, sin)


def _swa_kernel(sink_ref, q_ref, kc_ref, kp_ref, vc_ref, vp_ref, pq_ref, pkc_ref, pkp_ref, o_ref, *, layer):
    n = pl.program_id(1)
    blk = SWA_BLOCK
    pq = pq_ref[...]
    pk = jnp.concatenate([pkp_ref[0], pkc_ref[0]], axis=1)
    dist = jnp.abs(pq - pk).astype(F32)
    qi = lax.broadcasted_iota(jnp.int32, (blk, 2 * blk), 0)
    kj = lax.broadcasted_iota(jnp.int32, (blk, 2 * blk), 1)
    diff = qi + blk - kj
    valid = (diff >= 0) & (diff < blk) & ((n > 0) | (kj >= blk))
    low = lax.broadcasted_iota(jnp.int32, (1, LANES), 1) < SWA_HEAD_DIM
    scale = SWA_HEAD_DIM ** -0.5

    for t in range(SWA_KV_HEADS // 2):
        cols = slice(t * LANES, (t + 1) * LANES)
        k2 = jnp.concatenate([kp_ref[:, cols], kc_ref[:, cols]], axis=0)
        v2 = jnp.concatenate([vp_ref[:, cols], vc_ref[:, cols]], axis=0)
        qs = jnp.concatenate(
            [q_ref[:, (4 * t + i) * LANES:(4 * t + i + 1) * LANES] for i in range(SWA_GROUP)], axis=0)
        outs = []
        for half in range(2):
            g = 2 * t + half
            keep = low if half == 0 else jnp.logical_not(low)
            qm = jnp.where(keep, qs, jnp.zeros_like(qs))
            s_all = lax.dot_general(qm, k2, NT_DIMS, preferred_element_type=F32)
            probs = []
            for i in range(SWA_GROUP):
                hq = SWA_GROUP * g + i
                slope = float(2.0 ** (-8.0 * (hq + 1) / SWA_Q_HEADS))
                s = s_all[i * blk:(i + 1) * blk] * scale - slope * dist
                s = jnp.where(valid, s, -jnp.inf)
                sink = sink_ref[layer, hq]
                m = jnp.maximum(jnp.max(s, axis=-1, keepdims=True), sink)
                e = jnp.exp(s - m)
                den = jnp.sum(e, axis=-1, keepdims=True) + jnp.exp(sink - m)
                probs.append((e / den).astype(BF16))
            p = jnp.concatenate(probs, axis=0)
            outs.append(jnp.dot(p, v2, preferred_element_type=F32))
        merged = jnp.where(low, outs[0], outs[1])
        for i in range(SWA_GROUP):
            o_ref[:, (4 * t + i) * LANES:(4 * t + i + 1) * LANES] = merged[i * blk:(i + 1) * blk]


def _swa(sinks, qa, ka, va, pos_col, pos_row, layer, *, B, S):
    T = qa.shape[0]
    blk = SWA_BLOCK
    nb = S // blk
    nq = SWA_Q_HEADS * SWA_HEAD_DIM
    nkv = SWA_KV_HEADS * SWA_HEAD_DIM
    cur = lambda b, n: (b * nb + n, 0)
    prev = lambda b, n: (b * nb + jnp.maximum(n - 1, 0), 0)
    cur3 = lambda b, n: (b * nb + n, 0, 0)
    prev3 = lambda b, n: (b * nb + jnp.maximum(n - 1, 0), 0, 0)
    return pl.pallas_call(
        functools.partial(_swa_kernel, layer=layer),
        grid=(B, nb),
        in_specs=[
            pl.BlockSpec(memory_space=pltpu.SMEM),
            pl.BlockSpec((blk, nq), cur),
            pl.BlockSpec((blk, nkv), cur),
            pl.BlockSpec((blk, nkv), prev),
            pl.BlockSpec((blk, nkv), cur),
            pl.BlockSpec((blk, nkv), prev),
            pl.BlockSpec((blk, 1), cur),
            pl.BlockSpec((1, 1, blk), cur3),
            pl.BlockSpec((1, 1, blk), prev3),
        ],
        out_specs=pl.BlockSpec((blk, nq), cur),
        out_shape=jax.ShapeDtypeStruct((T, nq), F32),
        compiler_params=_params(("parallel", "arbitrary")),
        name="swa_attn",
    )(sinks, qa, ka, ka, va, va, pos_col, pos_row, pos_row)


MLA_TILE = 1024
MLA_SUB = 256


def _softmax_chain(s_ref, rows, nkeys, diag, m_ref, l_ref, acc_ref):
    blocks = [s_ref[rows, b * LANES:(b + 1) * LANES] for b in range(nkeys // LANES)]
    if diag:
        row = lax.broadcasted_iota(jnp.int32, (MLA_SUB, LANES), 0)
        lane = lax.broadcasted_iota(jnp.int32, (MLA_SUB, LANES), 1)
        nd = MLA_SUB // LANES
        for d in range(nd):
            b = len(blocks) - nd + d
            blocks[b] = jnp.where(lane + d * LANES <= row, blocks[b], -jnp.inf)
    mx = functools.reduce(jnp.maximum, blocks)
    m_prev = m_ref[rows, :]
    m_new = jnp.maximum(m_prev, jnp.max(mx, axis=-1, keepdims=True))
    alpha = jnp.exp2(m_prev - m_new)
    ps = [jnp.exp2(b - m_new) for b in blocks]
    l_ref[rows, :] = alpha * l_ref[rows, :] + functools.reduce(jnp.add, ps)
    acc_ref[rows, :] = alpha * acc_ref[rows, :]
    m_ref[rows, :] = m_new
    return jnp.concatenate([pb.astype(BF16) for pb in ps], axis=-1)


def _mla_kernel(q_ref, k_ref, v_ref, o_ref, s0_ref, s1_ref, m_ref, l_ref, acc_ref):
    i = pl.program_id(2)
    nchain = MLA_TILE // MLA_SUB
    rows = [slice(c * MLA_SUB, (c + 1) * MLA_SUB) for c in range(nchain)]
    m_ref[...] = jnp.full(m_ref.shape, -jnp.inf, F32)
    l_ref[...] = jnp.zeros(l_ref.shape, F32)
    acc_ref[...] = jnp.zeros(acc_ref.shape, F32)

    def chunk_rows(chunk):
        return pl.ds(pl.multiple_of(chunk * MLA_TILE, MLA_TILE), MLA_TILE)

    def scores(chunk, s_ref):
        k = k_ref[0, 0, chunk_rows(chunk), :]
        s_ref[...] = lax.dot_general(q_ref[0, 0], k, NT_DIMS, preferred_element_type=F32)

    def full_chunk(chunk, s_ref):
        p = jnp.concatenate(
            [_softmax_chain(s_ref, rows[c], MLA_TILE, False, m_ref, l_ref, acc_ref) for c in range(nchain)], axis=0)
        acc_ref[...] += jnp.dot(p, v_ref[0, 0, chunk_rows(chunk), :], preferred_element_type=F32)

    def diag_chunk(s_ref):
        base = pl.multiple_of(i * MLA_TILE, MLA_TILE)
        for c in range(nchain):
            nkeys = (c + 1) * MLA_SUB
            p = _softmax_chain(s_ref, rows[c], nkeys, True, m_ref, l_ref, acc_ref)
            acc_ref[rows[c], :] += jnp.dot(p, v_ref[0, 0, pl.ds(base, nkeys), :], preferred_element_type=F32)

    scores(0, s0_ref)

    def pair(jj, carry):
        scores(2 * jj + 1, s1_ref)
        full_chunk(2 * jj, s0_ref)
        scores(2 * jj + 2, s0_ref)
        full_chunk(2 * jj + 1, s1_ref)
        return carry

    lax.fori_loop(0, i // 2, pair, 0)

    @pl.when(i % 2 == 0)
    def _():
        diag_chunk(s0_ref)

    @pl.when(i % 2 == 1)
    def _():
        scores(i, s1_ref)
        full_chunk(i - 1, s0_ref)
        diag_chunk(s1_ref)

    l = jnp.sum(l_ref[...], axis=-1, keepdims=True)
    o_ref[0] = acc_ref[...] / l


def _mla(qb, kb, vb):
    B, H, S, _ = qb.shape
    assert S % MLA_TILE == 0 and MLA_TILE % MLA_SUB == 0
    return pl.pallas_call(
        _mla_kernel,
        grid=(B, H, S // MLA_TILE),
        in_specs=[
            pl.BlockSpec((1, 1, MLA_TILE, MLA_QK_PAD), lambda b, h, i: (b, h, i, 0)),
            pl.BlockSpec((1, 1, S, MLA_QK_PAD), lambda b, h, i: (b, h, 0, 0)),
            pl.BlockSpec((1, 1, S, MLA_V), lambda b, h, i: (b, h, 0, 0)),
        ],
        out_specs=pl.BlockSpec((1, MLA_TILE, MLA_V), lambda b, h, i: (b, i, h)),
        out_shape=jax.ShapeDtypeStruct((B, S, H * MLA_V), F32),
        scratch_shapes=[pltpu.VMEM((MLA_TILE, MLA_TILE), F32), pltpu.VMEM((MLA_TILE, MLA_TILE), F32),
                        pltpu.VMEM((MLA_TILE, LANES), F32), pltpu.VMEM((MLA_TILE, LANES), F32),
                        pltpu.VMEM((MLA_TILE, MLA_V), F32)],
        compiler_params=_params(("parallel", "parallel", "arbitrary")),
        name="mla_attn",
    )(qb, kb, vb)


def _outproj_kernel(h_ref, oa_ref, ob_ref, ga_ref, gb_ref, woa_ref, wob_ref, o_ref):
    a = _rms(oa_ref[...], ga_ref[...]).astype(BF16)
    b = _rms(ob_ref[...], gb_ref[...]).astype(BF16)
    o_ref[...] = (h_ref[...] + jnp.dot(a, woa_ref[...], preferred_element_type=F32)
                  + jnp.dot(b, wob_ref[...], preferred_element_type=F32))


def _outproj(h, oa, ob, ga, gb, woa, wob, layer, *, tm=512):
    T, D = h.shape
    row = lambda w: pl.BlockSpec((tm, w), lambda i: (i, 0))
    full = lambda a: _layer_spec(a, layer)
    return pl.pallas_call(
        _outproj_kernel,
        grid=(T // tm,),
        in_specs=[row(D), row(oa.shape[1]), row(ob.shape[1]), full(ga), full(gb), full(woa), full(wob)],
        out_specs=row(D),
        out_shape=jax.ShapeDtypeStruct((T, D), F32),
        compiler_params=_params(("parallel",)),
        name="outproj",
    )(h, oa, ob, ga, gb, woa, wob)


def _ple_kernel(h_ref, p_ref, gg_ref, wgate_ref, wp_ref, gp_ref, o_ref):
    h = h_ref[...]
    gl = jnp.dot(_rms(h, gg_ref[...]).astype(BF16), wgate_ref[...], preferred_element_type=F32)
    gate = 1.0 / (1.0 + jnp.exp(-gl))
    pe = jnp.dot(p_ref[...].astype(BF16), wp_ref[...], preferred_element_type=F32)
    o_ref[...] = h + _rms(pe, gp_ref[...]) * gate


def _ple(h, p, gg, wgate, wp, gp, layer, *, tm=512):
    T, D = h.shape
    row = lambda w: pl.BlockSpec((tm, w), lambda i: (i, 0))
    full = lambda a: _layer_spec(a, layer)
    return pl.pallas_call(
        _ple_kernel,
        grid=(T // tm,),
        in_specs=[row(D), pl.BlockSpec((None, tm, p.shape[2]), lambda i: (layer, i, 0)),
                  full(gg), full(wgate), full(wp), full(gp)],
        out_specs=row(D),
        out_shape=jax.ShapeDtypeStruct((T, D), F32),
        compiler_params=_params(("parallel",)),
        name="ple",
    )(h, p, gg, wgate, wp, gp)


def _pad_rope(a):
    half = MLA_ROPE // 2
    zeros = jnp.zeros(a.shape[:-1] + (LANES // 2 - half,), a.dtype)
    return jnp.concatenate([a[..., :half], zeros, a[..., half:], zeros], axis=-1)


def _pair_heads(a, axis):
    heads = [8 * t + 4 * half + i for t in range(SWA_KV_HEADS // 2) for i in range(SWA_GROUP) for half in range(2)]
    return jnp.concatenate(
        [lax.slice_in_dim(a, SWA_HEAD_DIM * h, SWA_HEAD_DIM * (h + 1), axis=axis) for h in heads], axis=axis)


def kernel(x, p, positions, ffn1_norm, ffn1_w_gate, ffn1_w_up, ffn1_w_down, mix_norm, w_in, swa_q_norm, swa_k_norm, swa_sinks, mla_q_lora_norm, mla_w_uq, mla_kv_lora_norm, mla_w_ukv, mla_q_norm, mla_k_norm, out_norm_swa, out_norm_mla, w_o, ffn2_norm, ffn2_w_gate, ffn2_w_up, ffn2_w_down, ple_proj, ple_proj_norm, ple_gate_norm, ple_gate):
    B, S, D = x.shape
    depth = p.shape[0]
    T = B * S
    nq = SWA_Q_HEADS * SWA_HEAD_DIM
    nkv = SWA_KV_HEADS * SWA_HEAD_DIM
    o_kr = nq + 2 * nkv + MLA_Q_RANK + MLA_KV_RANK

    w1g, w1u, w1d = ffn1_w_gate, ffn1_w_up, ffn1_w_down
    w2g, w2u, w2d = ffn2_w_gate, ffn2_w_up, ffn2_w_down
    win = jnp.concatenate(
        [_pair_heads(w_in[..., :nq], -1), w_in[..., nq:o_kr], _pad_rope(w_in[..., o_kr:])], axis=-1).astype(BF16)
    wuq = jnp.concatenate(
        [mla_w_uq[..., MLA_QK * h:MLA_QK * h + MLA_NOPE] for h in range(MLA_HEADS)]
        + [_pad_rope(mla_w_uq[..., MLA_QK * h + MLA_NOPE:MLA_QK * (h + 1)]) for h in range(MLA_HEADS)],
        axis=-1).astype(BF16)
    wukv = mla_w_ukv.astype(BF16)
    woa = _pair_heads(w_o[:, :nq], 1).astype(BF16)
    wob = w_o[:, nq:].astype(BF16)
    wgate = ple_gate.astype(BF16)
    wp = ple_proj.astype(BF16)

    row = lambda v: v.reshape(depth, 1, -1)
    gqa = row(jnp.tile(swa_q_norm, (1, 2)))
    gka = row(jnp.tile(swa_k_norm, (1, 2)))
    gqn, gqr = row(mla_q_norm[:, :MLA_NOPE]), row(_pad_rope(mla_q_norm[:, MLA_NOPE:]))
    gkn, gkr = row(mla_k_norm[:, :MLA_NOPE]), row(_pad_rope(mla_k_norm[:, MLA_NOPE:]))
    ga, gb = row(_pair_heads(out_norm_swa, 1)), row(out_norm_mla)
    g1, g2, gmix = row(ffn1_norm), row(ffn2_norm), row(mix_norm)
    gcq, gckv = row(mla_q_lora_norm), row(mla_kv_lora_norm)
    gpg, gpp = row(ple_gate_norm), row(ple_proj_norm)

    half = MLA_ROPE // 2
    inv_freq = ROPE_THETA ** (-jnp.arange(half, dtype=F32) / half)
    inv_full = _pad_rope(jnp.concatenate([inv_freq, inv_freq])).reshape(1, LANES)
    sign_full = _pad_rope(jnp.concatenate([-jnp.ones((half,), F32), jnp.ones((half,), F32)])).reshape(1, LANES)
    pos_col = positions.reshape(T, 1)
    pos_row = positions.reshape(T // SWA_BLOCK, 1, SWA_BLOCK)
    cos_t, sin_t = _rope_tables(pos_col, inv_full, sign_full)

    pf = p.reshape(depth, T, -1)
    xf = x.reshape(T, D)
    for i in range(depth):
        h = _ffn(xf, g1, w1g, w1u, w1d, i)
        qa, ka, va, qb, kb, vb = _inproj(h, gmix, win, gqa, gka, gcq, wuq, gckv, wukv, gqn, gqr, gkn, gkr,
                                         cos_t, sin_t, i, B=B, S=S)
        oa = _swa(swa_sinks, qa, ka, va, pos_col, pos_row, i, B=B, S=S)
        ob = _mla(qb, kb, vb).reshape(T, MLA_HEADS * MLA_V)
        h = _outproj(h, oa, ob, ga, gb, woa, wob, i)
        h = _ffn(h, g2, w2g, w2u, w2d, i)
        xf = _ple(h, pf, gpg, wgate, wp, gpp, i)
    return xf.reshape(B, S, D)
```

```python
import functools
import math

import jax
import jax.numpy as jnp
from jax import lax
from jax.experimental import pallas as pl
from jax.experimental.pallas import tpu as pltpu

F32 = jnp.float32
BF16 = jnp.bfloat16

EPS = 1e-6
ROPE_THETA = 10000.0
LANES = 128
SWA_BLOCK = 128
SWA_HEAD_DIM = 64
SWA_Q_HEADS = 16
SWA_KV_HEADS = 4
SWA_GROUP = 4
MLA_HEADS = 8
MLA_NOPE = 128
MLA_ROPE = 64
MLA_QK = MLA_NOPE + MLA_ROPE
MLA_V = 128
MLA_Q_RANK = 512
MLA_KV_RANK = 256
MLA_QK_PAD = 2 * LANES
LOG2E = math.log2(math.e)
MLA_Q_PRESCALE = (MLA_QK ** -0.5) * LOG2E
SWA_Q_PRESCALE = (SWA_HEAD_DIM ** -0.5) * LOG2E
VMEM_LIMIT = 56 * 1024 * 1024
FFN_VMEM_LIMIT = 60 * 1024 * 1024

NT_DIMS = (((1,), (1,)), ((), ()))


def _rms(x, g):
    ms = jnp.mean(x * x, axis=-1, keepdims=True)
    return x * lax.rsqrt(ms + EPS) * g


def _params(sem, vmem=VMEM_LIMIT):
    return pltpu.CompilerParams(dimension_semantics=sem, vmem_limit_bytes=vmem)


def _layer_spec(a, layer):
    nd = a.ndim - 1
    return pl.BlockSpec((None,) + a.shape[1:], lambda *_: (layer,) + (0,) * nd, pipeline_mode=pl.Buffered(1))


def _ffn_kernel(x_ref, g_ref, wg_ref, wu_ref, wd_ref, o_ref, xn_ref):
    @pl.when(pl.program_id(1) == 0)
    def _():
        x = x_ref[...]
        xn_ref[...] = _rms(x, g_ref[...]).astype(BF16)
        o_ref[...] = x

    xn = xn_ref[...]
    gate = jnp.dot(xn, wg_ref[...].astype(BF16), preferred_element_type=F32)
    up = jnp.dot(xn, wu_ref[...].astype(BF16), preferred_element_type=F32)
    h = (0.5 * gate / (1.0 + jnp.exp(-gate)) * up).astype(BF16)
    o_ref[...] += jnp.dot(h, wd_ref[...].astype(BF16), preferred_element_type=F32)


def _ffn(x, g, wg, wu, wd, layer, *, tm=1024, tf=512):
    T, D = x.shape
    Fd = wg.shape[2]
    assert T % tm == 0 and Fd % tf == 0
    return pl.pallas_call(
        _ffn_kernel,
        grid=(T // tm, Fd // tf),
        in_specs=[
            pl.BlockSpec((tm, D), lambda i, f: (i, 0), pipeline_mode=pl.Buffered(1)),
            _layer_spec(g, layer),
            pl.BlockSpec((None, D, tf), lambda i, f: (layer, 0, f)),
            pl.BlockSpec((None, D, tf), lambda i, f: (layer, 0, f)),
            pl.BlockSpec((None, tf, D), lambda i, f: (layer, f, 0)),
        ],
        out_specs=pl.BlockSpec((tm, D), lambda i, f: (i, 0)),
        out_shape=jax.ShapeDtypeStruct((T, D), F32),
        scratch_shapes=[pltpu.VMEM((tm, D), BF16)],
        compiler_params=_params(("parallel", "arbitrary"), FFN_VMEM_LIMIT),
        name="ffn",
    )(x, g, wg, wu, wd)


def _rope_table_kernel(pos_ref, inv_ref, sign_ref, cos_ref, sin_ref):
    ang = pos_ref[...].astype(F32) * inv_ref[...]
    cos_ref[...] = jnp.cos(ang)
    sin_ref[...] = jnp.sin(ang) * sign_ref[...]


def _rope_tables(pos_col, inv_full, sign_full, *, tm=2048):
    T = pos_col.shape[0]
    return pl.pallas_call(
        _rope_table_kernel,
        grid=(T // tm,),
        in_specs=[
            pl.BlockSpec((tm, 1), lambda i: (i, 0)),
            pl.BlockSpec((1, LANES), lambda i: (0, 0)),
            pl.BlockSpec((1, LANES), lambda i: (0, 0)),
        ],
        out_specs=[pl.BlockSpec((tm, LANES), lambda i: (i, 0))] * 2,
        out_shape=[jax.ShapeDtypeStruct((T, LANES), F32)] * 2,
        compiler_params=_params(("parallel",)),
        name="rope_tables",
    )(pos_col, inv_full, sign_full)


def _inproj_kernel(x_ref, gmix_ref, win_ref, gqa_ref, gka_ref, gcq_ref, wuq_ref, gckv_ref, wukv_ref,
                   gqn_ref, gqr_ref, gkn_ref, gkr_ref, cos_ref, sin_ref,
                   qa_ref, ka_ref, va_ref, qb_ref, kb_ref, vb_ref):
    n = _rms(x_ref[...], gmix_ref[...]).astype(BF16)
    o_lat = SWA_Q_HEADS * SWA_HEAD_DIM + 2 * SWA_KV_HEADS * SWA_HEAD_DIM
    z_lat = jnp.dot(n, win_ref[:, o_lat:], preferred_element_type=F32)
    z_swa = jnp.dot(n, win_ref[:, :o_lat], preferred_element_type=F32)
    z = jnp.concatenate([z_swa, z_lat], axis=-1)

    low = lax.broadcasted_iota(jnp.int32, (1, LANES), 1) < SWA_HEAD_DIM

    def head_pair_norm(blk, g):
        sq = blk * blk
        s_lo = jnp.sum(jnp.where(low, sq, 0.0), axis=-1, keepdims=True)
        s_hi = jnp.sum(jnp.where(low, 0.0, sq), axis=-1, keepdims=True)
        r = jnp.where(low, lax.rsqrt(s_lo / SWA_HEAD_DIM + EPS), lax.rsqrt(s_hi / SWA_HEAD_DIM + EPS))
        return blk * r * g

    nq = SWA_Q_HEADS * SWA_HEAD_DIM
    nkv = SWA_KV_HEADS * SWA_HEAD_DIM
    for c in range(nq // LANES):
        qa_ref[:, c * LANES:(c + 1) * LANES] = (head_pair_norm(
            z[:, c * LANES:(c + 1) * LANES], gqa_ref[...]) * SWA_Q_PRESCALE).astype(BF16)
    for c in range(nkv // LANES):
        ka_ref[:, c * LANES:(c + 1) * LANES] = head_pair_norm(
            z[:, nq + c * LANES:nq + (c + 1) * LANES], gka_ref[...]).astype(BF16)
    va_ref[...] = z[:, nq + nkv:nq + 2 * nkv].astype(BF16)

    o_cq = nq + 2 * nkv
    o_ckv = o_cq + MLA_Q_RANK
    o_kr = o_ckv + MLA_KV_RANK
    cq = _rms(z[:, o_cq:o_ckv], gcq_ref[...]).astype(BF16)
    qb = jnp.dot(cq, wuq_ref[...], preferred_element_type=F32)
    ckv = _rms(z[:, o_ckv:o_kr], gckv_ref[...]).astype(BF16)
    kv = jnp.dot(ckv, wukv_ref[...], preferred_element_type=F32)
    kr = z[:, o_kr:o_kr + LANES]

    cos = cos_ref[...]
    sin = sin_ref[...]

    def rope(t):
        return t * cos + pltpu.roll(t, LANES // 2, 1) * sin

    kr_rot = rope(kr * gkr_ref[...])
    ssq_kr = jnp.sum(kr * kr, axis=-1, keepdims=True)
    for h in range(MLA_HEADS):
        qn = qb[:, h * LANES:(h + 1) * LANES]
        qr = qb[:, (MLA_HEADS + h) * LANES:(MLA_HEADS + h + 1) * LANES]
        ssq = jnp.sum(qn * qn, axis=-1, keepdims=True) + jnp.sum(qr * qr, axis=-1, keepdims=True)
        r = lax.rsqrt(ssq / MLA_QK + EPS)
        qb_ref[0, h, :, 0:LANES] = (qn * r * gqn_ref[...] * MLA_Q_PRESCALE).astype(BF16)
        qb_ref[0, h, :, LANES:2 * LANES] = (rope(qr * r * gqr_ref[...]) * MLA_Q_PRESCALE).astype(BF16)
        kn = kv[:, 2 * h * LANES:(2 * h + 1) * LANES]
        vh = kv[:, (2 * h + 1) * LANES:(2 * h + 2) * LANES]
        rk = lax.rsqrt((jnp.sum(kn * kn, axis=-1, keepdims=True) + ssq_kr) / MLA_QK + EPS)
        kb_ref[0, h, :, 0:LANES] = (kn * rk * gkn_ref[...]).astype(BF16)
        kb_ref[0, h, :, LANES:2 * LANES] = (kr_rot * rk).astype(BF16)
        vb_ref[0, h] = vh.astype(BF16)


def _inproj(x, gmix, win, gqa, gka, gcq, wuq, gckv, wukv, gqn, gqr, gkn, gkr, cos, sin, layer, *, B, S, tm=512):
    T, D = x.shape
    ns = S // tm
    nq = SWA_Q_HEADS * SWA_HEAD_DIM
    nkv = SWA_KV_HEADS * SWA_HEAD_DIM
    full = lambda a: _layer_spec(a, layer)

    row = lambda w: pl.BlockSpec((tm, w), lambda i: (i, 0))
    head = lambda w: pl.BlockSpec((1, MLA_HEADS, tm, w), lambda i: (i // ns, 0, i % ns, 0))
    return pl.pallas_call(
        _inproj_kernel,
        grid=(T // tm,),
        in_specs=[row(D), full(gmix), full(win), full(gqa), full(gka), full(gcq), full(wuq), full(gckv),
                  full(wukv), full(gqn), full(gqr), full(gkn), full(gkr), row(LANES), row(LANES)],
        out_specs=[row(nq), row(nkv), row(nkv), head(MLA_QK_PAD), head(MLA_QK_PAD), head(MLA_V)],
        out_shape=[
            jax.ShapeDtypeStruct((T, nq), BF16),
            jax.ShapeDtypeStruct((T, nkv), BF16),
            jax.ShapeDtypeStruct((T, nkv), BF16),
            jax.ShapeDtypeStruct((B, MLA_HEADS, S, MLA_QK_PAD), BF16),
            jax.ShapeDtypeStruct((B, MLA_HEADS, S, MLA_QK_PAD), BF16),
            jax.ShapeDtypeStruct((B, MLA_HEADS, S, MLA_V), BF16),
        ],
        compiler_params=_params(("parallel",)),
        name="inproj",
    )(x, gmix, win, gqa, gka, gcq, wuq, gckv, wukv, gqn, gqr, gkn, gkr, cos, sin)


def _swa_kernel(sink_ref, q_ref, kc_ref, kp_ref, vc_ref, vp_ref, pq_ref, pkc_ref, pkp_ref, o_ref, *, layer):
    n = pl.program_id(1)
    blk = SWA_BLOCK
    pq = pq_ref[...]
    pk = jnp.concatenate([pkp_ref[0], pkc_ref[0]], axis=1)
    qi = lax.broadcasted_iota(jnp.int32, (blk, 2 * blk), 0)
    kj = lax.broadcasted_iota(jnp.int32, (blk, 2 * blk), 1)
    diff = qi + blk - kj
    valid = (diff >= 0) & (diff < blk) & ((n > 0) | (kj >= blk))
    neg_dist = jnp.where(valid, -jnp.abs(pq - pk).astype(F32), -jnp.inf)
    low = lax.broadcasted_iota(jnp.int32, (1, LANES), 1) < SWA_HEAD_DIM

    for t in range(SWA_KV_HEADS // 2):
        cols = slice(t * LANES, (t + 1) * LANES)
        k2 = jnp.concatenate([kp_ref[:, cols], kc_ref[:, cols]], axis=0)
        v2 = jnp.concatenate([vp_ref[:, cols], vc_ref[:, cols]], axis=0)
        qs = jnp.concatenate(
            [q_ref[:, (4 * t + i) * LANES:(4 * t + i + 1) * LANES] for i in range(SWA_GROUP)], axis=0)
        outs = []
        for half in range(2):
            g = 2 * t + half
            keep = low if half == 0 else jnp.logical_not(low)
            qm = jnp.where(keep, qs, jnp.zeros_like(qs))
            s_all = lax.dot_general(qm, k2, NT_DIMS, preferred_element_type=F32)
            probs, inv_den = [], []
            for i in range(SWA_GROUP):
                hq = SWA_GROUP * g + i
                slope2 = float(2.0 ** (-8.0 * (hq + 1) / SWA_Q_HEADS)) * LOG2E
                s = s_all[i * blk:(i + 1) * blk] + slope2 * neg_dist
                sink2 = sink_ref[layer, hq] * LOG2E
                m = jnp.maximum(jnp.max(s, axis=-1, keepdims=True), sink2)
                e = jnp.exp2(s - m)
                den = jnp.sum(e, axis=-1, keepdims=True) + jnp.exp2(sink2 - m)
                probs.append(e.astype(BF16))
                inv_den.append(1.0 / den)
            p = jnp.concatenate(probs, axis=0)
            o = jnp.dot(p, v2, preferred_element_type=F32)
            outs.append(jnp.concatenate(
                [o[i * blk:(i + 1) * blk] * inv_den[i] for i in range(SWA_GROUP)], axis=0))
        merged = jnp.where(low, outs[0], outs[1])
        for i in range(SWA_GROUP):
            o_ref[:, (4 * t + i) * LANES:(4 * t + i + 1) * LANES] = merged[i * blk:(i + 1) * blk]


def _swa(sinks, qa, ka, va, pos_col, pos_row, layer, *, B, S):
    T = qa.shape[0]
    blk = SWA_BLOCK
    nb = S // blk
    nq = SWA_Q_HEADS * SWA_HEAD_DIM
    nkv = SWA_KV_HEADS * SWA_HEAD_DIM
    cur = lambda b, n: (b * nb + n, 0)
    prev = lambda b, n: (b * nb + jnp.maximum(n - 1, 0), 0)
    cur3 = lambda b, n: (b * nb + n, 0, 0)
    prev3 = lambda b, n: (b * nb + jnp.maximum(n - 1, 0), 0, 0)
    return pl.pallas_call(
        functools.partial(_swa_kernel, layer=layer),
        grid=(B, nb),
        in_specs=[
            pl.BlockSpec(memory_space=pltpu.SMEM),
            pl.BlockSpec((blk, nq), cur),
            pl.BlockSpec((blk, nkv), cur),
            pl.BlockSpec((blk, nkv), prev),
            pl.BlockSpec((blk, nkv), cur),
            pl.BlockSpec((blk, nkv), prev),
            pl.BlockSpec((blk, 1), cur),
            pl.BlockSpec((1, 1, blk), cur3),
            pl.BlockSpec((1, 1, blk), prev3),
        ],
        out_specs=pl.BlockSpec((blk, nq), cur),
        out_shape=jax.ShapeDtypeStruct((T, nq), F32),
        compiler_params=_params(("parallel", "arbitrary")),
        name="swa_attn",
    )(sinks, qa, ka, ka, va, va, pos_col, pos_row, pos_row)


MLA_TILE = 1024
MLA_SUB = 256
MLA_HPS = 2


def _softmax_chain(s_ref, rows, nkeys, diag, m_ref, l_ref, acc_ref):
    blocks = [s_ref[rows, b * LANES:(b + 1) * LANES] for b in range(nkeys // LANES)]
    if diag:
        row = lax.broadcasted_iota(jnp.int32, (MLA_SUB, LANES), 0)
        lane = lax.broadcasted_iota(jnp.int32, (MLA_SUB, LANES), 1)
        nd = MLA_SUB // LANES
        for d in range(nd):
            b = len(blocks) - nd + d
            blocks[b] = jnp.where(lane + d * LANES <= row, blocks[b], -jnp.inf)
    mx = functools.reduce(jnp.maximum, blocks)
    m_prev = m_ref[rows, :]
    m_new = jnp.maximum(m_prev, jnp.max(mx, axis=-1, keepdims=True))
    alpha = jnp.exp2(m_prev - m_new)
    ps = [jnp.exp2(b - m_new) for b in blocks]
    l_ref[rows, :] = alpha * l_ref[rows, :] + functools.reduce(jnp.add, ps)
    acc_ref[rows, :] = alpha * acc_ref[rows, :]
    m_ref[rows, :] = m_new
    return jnp.concatenate([pb.astype(BF16) for pb in ps], axis=-1)


def _mla_kernel(q_ref, k_ref, v_ref, o_ref, s_ref, m_ref, l_ref, acc_ref):
    i = pl.program_id(2)
    heads = range(MLA_HPS)
    nchain = MLA_TILE // MLA_SUB
    rows = [slice(c * MLA_SUB, (c + 1) * MLA_SUB) for c in range(nchain)]
    m_ref[...] = jnp.full(m_ref.shape, -jnp.inf, F32)
    l_ref[...] = jnp.zeros(l_ref.shape, F32)
    acc_ref[...] = jnp.zeros(acc_ref.shape, F32)

    def chunk_rows(chunk):
        return pl.ds(pl.multiple_of(chunk * MLA_TILE, MLA_TILE), MLA_TILE)

    def scores(h, chunk, buf):
        k = k_ref[0, h, chunk_rows(chunk), :]
        s_ref[h, buf] = lax.dot_general(q_ref[0, h], k, NT_DIMS, preferred_element_type=F32)

    def state(h):
        return m_ref.at[h], l_ref.at[h], acc_ref.at[h]

    def full_chunk(h, chunk, buf):
        p = jnp.concatenate(
            [_softmax_chain(s_ref.at[h, buf], rows[c], MLA_TILE, False, *state(h)) for c in range(nchain)], axis=0)
        acc_ref[h] += jnp.dot(p, v_ref[0, h, chunk_rows(chunk), :], preferred_element_type=F32)

    def diag_chunk(h, buf):
        base = pl.multiple_of(i * MLA_TILE, MLA_TILE)
        for c in range(nchain):
            nkeys = (c + 1) * MLA_SUB
            p = _softmax_chain(s_ref.at[h, buf], rows[c], nkeys, True, *state(h))
            acc_ref[h, rows[c], :] += jnp.dot(p, v_ref[0, h, pl.ds(base, nkeys), :], preferred_element_type=F32)

    for h in heads:
        scores(h, 0, 0)

    def pair(jj, carry):
        for h in heads:
            scores(h, 2 * jj + 1, 1)
        for h in heads:
            full_chunk(h, 2 * jj, 0)
        for h in heads:
            scores(h, 2 * jj + 2, 0)
        for h in heads:
            full_chunk(h, 2 * jj + 1, 1)
        return carry

    lax.fori_loop(0, i // 2, pair, 0)

    @pl.when(i % 2 == 0)
    def _():
        for h in heads:
            diag_chunk(h, 0)

    @pl.when(i % 2 == 1)
    def _():
        for h in heads:
            scores(h, i, 1)
        for h in heads:
            full_chunk(h, i - 1, 0)
        for h in heads:
            diag_chunk(h, 1)

    for h in heads:
        l = jnp.sum(l_ref[h], axis=-1, keepdims=True)
        o_ref[0, :, h * MLA_V:(h + 1) * MLA_V] = acc_ref[h] / l


def _mla(qb, kb, vb):
    B, H, S, _ = qb.shape
    assert S % MLA_TILE == 0 and MLA_TILE % MLA_SUB == 0 and H % MLA_HPS == 0
    hp = MLA_HPS
    return pl.pallas_call(
        _mla_kernel,
        grid=(B, H // hp, S // MLA_TILE),
        in_specs=[
            pl.BlockSpec((1, hp, MLA_TILE, MLA_QK_PAD), lambda b, g, i: (b, g, i, 0)),
            pl.BlockSpec((1, hp, S, MLA_QK_PAD), lambda b, g, i: (b, g, 0, 0)),
            pl.BlockSpec((1, hp, S, MLA_V), lambda b, g, i: (b, g, 0, 0)),
        ],
        out_specs=pl.BlockSpec((1, MLA_TILE, hp * MLA_V), lambda b, g, i: (b, i, g)),
        out_shape=jax.ShapeDtypeStruct((B, S, H * MLA_V), F32),
        scratch_shapes=[pltpu.VMEM((hp, 2, MLA_TILE, MLA_TILE), F32),
                        pltpu.VMEM((hp, MLA_TILE, LANES), F32), pltpu.VMEM((hp, MLA_TILE, LANES), F32),
                        pltpu.VMEM((hp, MLA_TILE, MLA_V), F32)],
        compiler_params=_params(("parallel", "parallel", "arbitrary")),
        name="mla_attn",
    )(qb, kb, vb)


def _outproj_kernel(h_ref, oa_ref, ob_ref, ga_ref, gb_ref, woa_ref, wob_ref, o_ref):
    a = _rms(oa_ref[...], ga_ref[...]).astype(BF16)
    b = _rms(ob_ref[...], gb_ref[...]).astype(BF16)
    o_ref[...] = (h_ref[...] + jnp.dot(a, woa_ref[...], preferred_element_type=F32)
                  + jnp.dot(b, wob_ref[...], preferred_element_type=F32))


def _outproj(h, oa, ob, ga, gb, woa, wob, layer, *, tm=512):
    T, D = h.shape
    row = lambda w: pl.BlockSpec((tm, w), lambda i: (i, 0))
    full = lambda a: _layer_spec(a, layer)
    return pl.pallas_call(
        _outproj_kernel,
        grid=(T // tm,),
        in_specs=[row(D), row(oa.shape[1]), row(ob.shape[1]), full(ga), full(gb), full(woa), full(wob)],
        out_specs=row(D),
        out_shape=jax.ShapeDtypeStruct((T, D), F32),
        compiler_params=_params(("parallel",)),
        name="outproj",
    )(h, oa, ob, ga, gb, woa, wob)


def _ple_kernel(h_ref, p_ref, gg_ref, wgate_ref, wp_ref, gp_ref, o_ref):
    h = h_ref[...]
    pe = _rms(jnp.dot(p_ref[...].astype(BF16), wp_ref[...], preferred_element_type=F32), gp_ref[...])
    n = _rms(h, gg_ref[...]).astype(BF16)
    half = wgate_ref.shape[1] // 2
    for c in range(2):
        cols = slice(c * half, (c + 1) * half)
        gl = jnp.dot(n, wgate_ref[:, cols], preferred_element_type=F32)
        o_ref[:, cols] = h[:, cols] + pe[:, cols] / (1.0 + jnp.exp(-gl))


def _ple(h, p, gg, wgate, wp, gp, layer, *, tm=512):
    T, D = h.shape
    row = lambda w: pl.BlockSpec((tm, w), lambda i: (i, 0))
    full = lambda a: _layer_spec(a, layer)
    return pl.pallas_call(
        _ple_kernel,
        grid=(T // tm,),
        in_specs=[row(D), pl.BlockSpec((None, tm, p.shape[2]), lambda i: (layer, i, 0)),
                  full(gg), full(wgate), full(wp), full(gp)],
        out_specs=row(D),
        out_shape=jax.ShapeDtypeStruct((T, D), F32),
        compiler_params=_params(("parallel",)),
        name="ple",
    )(h, p, gg, wgate, wp, gp)


def _pad_rope(a):
    half = MLA_ROPE // 2
    zeros = jnp.zeros(a.shape[:-1] + (LANES // 2 - half,), a.dtype)
    return jnp.concatenate([a[..., :half], zeros, a[..., half:], zeros], axis=-1)


def _pair_heads(a, axis):
    heads = [8 * t + 4 * half + i for t in range(SWA_KV_HEADS // 2) for i in range(SWA_GROUP) for half in range(2)]
    return jnp.concatenate(
        [lax.slice_in_dim(a, SWA_HEAD_DIM * h, SWA_HEAD_DIM * (h + 1), axis=axis) for h in heads], axis=axis)


def kernel(x, p, positions, ffn1_norm, ffn1_w_gate, ffn1_w_up, ffn1_w_down, mix_norm, w_in, swa_q_norm, swa_k_norm, swa_sinks, mla_q_lora_norm, mla_w_uq, mla_kv_lora_norm, mla_w_ukv, mla_q_norm, mla_k_norm, out_norm_swa, out_norm_mla, w_o, ffn2_norm, ffn2_w_gate, ffn2_w_up, ffn2_w_down, ple_proj, ple_proj_norm, ple_gate_norm, ple_gate):
    B, S, D = x.shape
    depth = p.shape[0]
    T = B * S
    nq = SWA_Q_HEADS * SWA_HEAD_DIM
    nkv = SWA_KV_HEADS * SWA_HEAD_DIM
    o_kr = nq + 2 * nkv + MLA_Q_RANK + MLA_KV_RANK

    w1g, w1u, w1d = ffn1_w_gate, ffn1_w_up, ffn1_w_down
    w2g, w2u, w2d = ffn2_w_gate, ffn2_w_up, ffn2_w_down
    win = jnp.concatenate(
        [_pair_heads(w_in[..., :nq], -1), w_in[..., nq:o_kr], _pad_rope(w_in[..., o_kr:])], axis=-1).astype(BF16)
    wuq = jnp.concatenate(
        [mla_w_uq[..., MLA_QK * h:MLA_QK * h + MLA_NOPE] for h in range(MLA_HEADS)]
        + [_pad_rope(mla_w_uq[..., MLA_QK * h + MLA_NOPE:MLA_QK * (h + 1)]) for h in range(MLA_HEADS)],
        axis=-1).astype(BF16)
    wukv = mla_w_ukv.astype(BF16)
    woa = _pair_heads(w_o[:, :nq], 1).astype(BF16)
    wob = w_o[:, nq:].astype(BF16)
    wgate = ple_gate.astype(BF16)
    wp = ple_proj.astype(BF16)

    row = lambda v: v.reshape(depth, 1, -1)
    gqa = row(jnp.tile(swa_q_norm, (1, 2)))
    gka = row(jnp.tile(swa_k_norm, (1, 2)))
    gqn, gqr = row(mla_q_norm[:, :MLA_NOPE]), row(_pad_rope(mla_q_norm[:, MLA_NOPE:]))
    gkn, gkr = row(mla_k_norm[:, :MLA_NOPE]), row(_pad_rope(mla_k_norm[:, MLA_NOPE:]))
    ga, gb = row(_pair_heads(out_norm_swa, 1)), row(out_norm_mla)
    g1, g2, gmix = row(ffn1_norm), row(ffn2_norm), row(mix_norm)
    gcq, gckv = row(mla_q_lora_norm), row(mla_kv_lora_norm)
    gpg, gpp = row(ple_gate_norm), row(ple_proj_norm)

    half = MLA_ROPE // 2
    inv_freq = ROPE_THETA ** (-jnp.arange(half, dtype=F32) / half)
    inv_full = _pad_rope(jnp.concatenate([inv_freq, inv_freq])).reshape(1, LANES)
    sign_full = _pad_rope(jnp.concatenate([-jnp.ones((half,), F32), jnp.ones((half,), F32)])).reshape(1, LANES)
    pos_col = positions.reshape(T, 1)
    pos_row = positions.reshape(T // SWA_BLOCK, 1, SWA_BLOCK)
    cos_t, sin_t = _rope_tables(pos_col, inv_full, sign_full)

    pf = p.reshape(depth, T, -1)
    xf = x.reshape(T, D)
    for i in range(depth):
        h = _ffn(xf, g1, w1g, w1u, w1d, i)
        qa, ka, va, qb, kb, vb = _inproj(h, gmix, win, gqa, gka, gcq, wuq, gckv, wukv, gqn, gqr, gkn, gkr,
                                         cos_t, sin_t, i, B=B, S=S)
        oa = _swa(swa_sinks, qa, ka, va, pos_col, pos_row, i, B=B, S=S)
        ob = _mla(qb, kb, vb).reshape(T, MLA_HEADS * MLA_V)
        h = _outproj(h, oa, ob, ga, gb, woa, wob, i)
        h = _ffn(h, g2, w2g, w2u, w2d, i)
        xf = _ple(h, pf, gpg, wgate, wp, gpp, i)
    return xf.reshape(B, S, D)
```

```python
import functools
import math

import jax
import jax.numpy as jnp
from jax import lax
from jax.experimental import pallas as pl
from jax.experimental.pallas import tpu as pltpu

F32 = jnp.float32
BF16 = jnp.bfloat16

EPS = 1e-6
ROPE_THETA = 10000.0
LANES = 128
SWA_BLOCK = 128
SWA_HEAD_DIM = 64
SWA_Q_HEADS = 16
SWA_KV_HEADS = 4
SWA_GROUP = 4
MLA_HEADS = 8
MLA_NOPE = 128
MLA_ROPE = 64
MLA_QK = MLA_NOPE + MLA_ROPE
MLA_V = 128
MLA_Q_RANK = 512
MLA_KV_RANK = 256
MLA_QK_PAD = 2 * LANES
LOG2E = math.log2(math.e)
MLA_Q_PRESCALE = (MLA_QK ** -0.5) * LOG2E
SWA_Q_PRESCALE = (SWA_HEAD_DIM ** -0.5) * LOG2E
VMEM_LIMIT = 56 * 1024 * 1024
FFN_VMEM_LIMIT = 60 * 1024 * 1024

NT_DIMS = (((1,), (1,)), ((), ()))


def _rms(x, g):
    ms = jnp.mean(x * x, axis=-1, keepdims=True)
    return x * lax.rsqrt(ms + EPS) * g


def _params(sem, vmem=VMEM_LIMIT):
    return pltpu.CompilerParams(dimension_semantics=sem, vmem_limit_bytes=vmem)


def _layer_spec(a, layer):
    nd = a.ndim - 1
    return pl.BlockSpec((None,) + a.shape[1:], lambda *_: (layer,) + (0,) * nd, pipeline_mode=pl.Buffered(1))


def _ffn_kernel(x_hbm, g_ref, wg_ref, wu_ref, wd_ref, o_ref, xbuf_ref, xn_ref, x_sem, *, tm):
    i = pl.program_id(0)
    f = pl.program_id(1)

    def x_copy(tile):
        rows = pl.ds(pl.multiple_of(tile * tm, tm), tm)
        return pltpu.make_async_copy(x_hbm.at[rows, :], xbuf_ref, x_sem)

    @pl.when(jnp.logical_and(i == 0, f == 0))
    def _():
        x_copy(0).start()

    @pl.when(f == 0)
    def _():
        x_copy(i).wait()
        x = xbuf_ref[...]
        xn_ref[...] = _rms(x, g_ref[...]).astype(BF16)
        o_ref[...] = x

    @pl.when(jnp.logical_and(f == 1, i + 1 < pl.num_programs(0)))
    def _():
        x_copy(i + 1).start()

    xn = xn_ref[...]
    gate = jnp.dot(xn, wg_ref[...].astype(BF16), preferred_element_type=F32)
    up = jnp.dot(xn, wu_ref[...].astype(BF16), preferred_element_type=F32)
    h = (0.5 * gate / (1.0 + jnp.exp(-gate)) * up).astype(BF16)
    o_ref[...] += jnp.dot(h, wd_ref[...].astype(BF16), preferred_element_type=F32)


def _ffn(x, g, wg, wu, wd, layer, *, tm=1024, tf=512):
    T, D = x.shape
    Fd = wg.shape[2]
    assert T % tm == 0 and Fd % tf == 0 and Fd // tf >= 2
    return pl.pallas_call(
        functools.partial(_ffn_kernel, tm=tm),
        grid=(T // tm, Fd // tf),
        in_specs=[
            pl.BlockSpec(memory_space=pl.ANY),
            _layer_spec(g, layer),
            pl.BlockSpec((None, D, tf), lambda i, f: (layer, 0, f)),
            pl.BlockSpec((None, D, tf), lambda i, f: (layer, 0, f)),
            pl.BlockSpec((None, tf, D), lambda i, f: (layer, f, 0)),
        ],
        out_specs=pl.BlockSpec((tm, D), lambda i, f: (i, 0)),
        out_shape=jax.ShapeDtypeStruct((T, D), F32),
        scratch_shapes=[pltpu.VMEM((tm, D), F32), pltpu.VMEM((tm, D), BF16), pltpu.SemaphoreType.DMA(())],
        compiler_params=_params(("arbitrary", "arbitrary"), FFN_VMEM_LIMIT),
        name="ffn",
    )(x, g, wg, wu, wd)


def _rope_table_kernel(pos_ref, inv_ref, sign_ref, cos_ref, sin_ref):
    ang = pos_ref[...].astype(F32) * inv_ref[...]
    cos_ref[...] = jnp.cos(ang)
    sin_ref[...] = jnp.sin(ang) * sign_ref[...]


def _rope_tables(pos_col, inv_full, sign_full, *, tm=2048):
    T = pos_col.shape[0]
    return pl.pallas_call(
        _rope_table_kernel,
        grid=(T // tm,),
        in_specs=[
            pl.BlockSpec((tm, 1), lambda i: (i, 0)),
            pl.BlockSpec((1, LANES), lambda i: (0, 0)),
            pl.BlockSpec((1, LANES), lambda i: (0, 0)),
        ],
        out_specs=[pl.BlockSpec((tm, LANES), lambda i: (i, 0))] * 2,
        out_shape=[jax.ShapeDtypeStruct((T, LANES), F32)] * 2,
        compiler_params=_params(("parallel",)),
        name="rope_tables",
    )(pos_col, inv_full, sign_full)


def _inproj_kernel(x_ref, gmix_ref, win_ref, gqa_ref, gka_ref, gcq_ref, wuq_ref, gckv_ref, wukv_ref,
                   gqn_ref, gqr_ref, gkn_ref, gkr_ref, cos_ref, sin_ref,
                   qa_ref, ka_ref, va_ref, qb_ref, kb_ref, vb_ref):
    n = _rms(x_ref[...], gmix_ref[...]).astype(BF16)
    o_lat = SWA_Q_HEADS * SWA_HEAD_DIM + 2 * SWA_KV_HEADS * SWA_HEAD_DIM
    z_lat = jnp.dot(n, win_ref[:, o_lat:], preferred_element_type=F32)
    z_swa = jnp.dot(n, win_ref[:, :o_lat], preferred_element_type=F32)
    z = jnp.concatenate([z_swa, z_lat], axis=-1)

    low = lax.broadcasted_iota(jnp.int32, (1, LANES), 1) < SWA_HEAD_DIM

    def head_pair_norm(blk, g):
        sq = blk * blk
        s_lo = jnp.sum(jnp.where(low, sq, 0.0), axis=-1, keepdims=True)
        s_hi = jnp.sum(jnp.where(low, 0.0, sq), axis=-1, keepdims=True)
        r = jnp.where(low, lax.rsqrt(s_lo / SWA_HEAD_DIM + EPS), lax.rsqrt(s_hi / SWA_HEAD_DIM + EPS))
        return blk * r * g

    nq = SWA_Q_HEADS * SWA_HEAD_DIM
    nkv = SWA_KV_HEADS * SWA_HEAD_DIM
    for c in range(nq // LANES):
        qa_ref[:, c * LANES:(c + 1) * LANES] = (head_pair_norm(
            z[:, c * LANES:(c + 1) * LANES], gqa_ref[...]) * SWA_Q_PRESCALE).astype(BF16)
    for c in range(nkv // LANES):
        ka_ref[:, c * LANES:(c + 1) * LANES] = head_pair_norm(
            z[:, nq + c * LANES:nq + (c + 1) * LANES], gka_ref[...]).astype(BF16)
    va_ref[...] = z[:, nq + nkv:nq + 2 * nkv].astype(BF16)

    o_cq = nq + 2 * nkv
    o_ckv = o_cq + MLA_Q_RANK
    o_kr = o_ckv + MLA_KV_RANK
    cq = _rms(z[:, o_cq:o_ckv], gcq_ref[...]).astype(BF16)
    qb = jnp.dot(cq, wuq_ref[...], preferred_element_type=F32)
    ckv = _rms(z[:, o_ckv:o_kr], gckv_ref[...]).astype(BF16)
    kv = jnp.dot(ckv, wukv_ref[...], preferred_element_type=F32)
    kr = z[:, o_kr:o_kr + LANES]

    cos = cos_ref[...]
    sin = sin_ref[...]

    def rope(t):
        return t * cos + pltpu.roll(t, LANES // 2, 1) * sin

    kr_rot = rope(kr * gkr_ref[...])
    ssq_kr = jnp.sum(kr * kr, axis=-1, keepdims=True)
    for h in range(MLA_HEADS):
        qn = qb[:, h * LANES:(h + 1) * LANES]
        qr = qb[:, (MLA_HEADS + h) * LANES:(MLA_HEADS + h + 1) * LANES]
        ssq = jnp.sum(qn * qn, axis=-1, keepdims=True) + jnp.sum(qr * qr, axis=-1, keepdims=True)
        r = lax.rsqrt(ssq / MLA_QK + EPS)
        qb_ref[0, h, :, 0:LANES] = (qn * r * gqn_ref[...] * MLA_Q_PRESCALE).astype(BF16)
        qb_ref[0, h, :, LANES:2 * LANES] = (rope(qr * r * gqr_ref[...]) * MLA_Q_PRESCALE).astype(BF16)
        kn = kv[:, 2 * h * LANES:(2 * h + 1) * LANES]
        vh = kv[:, (2 * h + 1) * LANES:(2 * h + 2) * LANES]
        rk = lax.rsqrt((jnp.sum(kn * kn, axis=-1, keepdims=True) + ssq_kr) / MLA_QK + EPS)
        kb_ref[0, h, :, 0:LANES] = (kn * rk * gkn_ref[...]).astype(BF16)
        kb_ref[0, h, :, LANES:2 * LANES] = (kr_rot * rk).astype(BF16)
        vb_ref[0, h] = vh.astype(BF16)


def _inproj(x, gmix, win, gqa, gka, gcq, wuq, gckv, wukv, gqn, gqr, gkn, gkr, cos, sin, layer, *, B, S, tm=512):
    T, D = x.shape
    ns = S // tm
    nq = SWA_Q_HEADS * SWA_HEAD_DIM
    nkv = SWA_KV_HEADS * SWA_HEAD_DIM
    full = lambda a: _layer_spec(a, layer)

    row = lambda w: pl.BlockSpec((tm, w), lambda i: (i, 0))
    head = lambda w: pl.BlockSpec((1, MLA_HEADS, tm, w), lambda i: (i // ns, 0, i % ns, 0))
    return pl.pallas_call(
        _inproj_kernel,
        grid=(T // tm,),
        in_specs=[row(D), full(gmix), full(win), full(gqa), full(gka), full(gcq), full(wuq), full(gckv),
                  full(wukv), full(gqn), full(gqr), full(gkn), full(gkr), row(LANES), row(LANES)],
        out_specs=[row(nq), row(nkv), row(nkv), head(MLA_QK_PAD), head(MLA_QK_PAD), head(MLA_V)],
        out_shape=[
            jax.ShapeDtypeStruct((T, nq), BF16),
            jax.ShapeDtypeStruct((T, nkv), BF16),
            jax.ShapeDtypeStruct((T, nkv), BF16),
            jax.ShapeDtypeStruct((B, MLA_HEADS, S, MLA_QK_PAD), BF16),
            jax.ShapeDtypeStruct((B, MLA_HEADS, S, MLA_QK_PAD), BF16),
            jax.ShapeDtypeStruct((B, MLA_HEADS, S, MLA_V), BF16),
        ],
        compiler_params=_params(("parallel",)),
        name="inproj",
    )(x, gmix, win, gqa, gka, gcq, wuq, gckv, wukv, gqn, gqr, gkn, gkr, cos, sin)


def _swa_kernel(sink_ref, q_ref, kc_ref, kp_ref, vc_ref, vp_ref, pq_ref, pkc_ref, pkp_ref, o_ref, *, layer):
    n = pl.program_id(1)
    blk = SWA_BLOCK
    pq = pq_ref[...]
    pk = jnp.concatenate([pkp_ref[0], pkc_ref[0]], axis=1)
    qi = lax.broadcasted_iota(jnp.int32, (blk, 2 * blk), 0)
    kj = lax.broadcasted_iota(jnp.int32, (blk, 2 * blk), 1)
    diff = qi + blk - kj
    valid = (diff >= 0) & (diff < blk) & ((n > 0) | (kj >= blk))
    neg_dist = jnp.where(valid, -jnp.abs(pq - pk).astype(F32), -jnp.inf)
    low = lax.broadcasted_iota(jnp.int32, (1, LANES), 1) < SWA_HEAD_DIM

    for t in range(SWA_KV_HEADS // 2):
        cols = slice(t * LANES, (t + 1) * LANES)
        k2 = jnp.concatenate([kp_ref[:, cols], kc_ref[:, cols]], axis=0)
        v2 = jnp.concatenate([vp_ref[:, cols], vc_ref[:, cols]], axis=0)
        qs = jnp.concatenate(
            [q_ref[:, (4 * t + i) * LANES:(4 * t + i + 1) * LANES] for i in range(SWA_GROUP)], axis=0)
        outs = []
        for half in range(2):
            g = 2 * t + half
            keep = low if half == 0 else jnp.logical_not(low)
            qm = jnp.where(keep, qs, jnp.zeros_like(qs))
            s_all = lax.dot_general(qm, k2, NT_DIMS, preferred_element_type=F32)
            probs, inv_den = [], []
            for i in range(SWA_GROUP):
                hq = SWA_GROUP * g + i
                slope2 = float(2.0 ** (-8.0 * (hq + 1) / SWA_Q_HEADS)) * LOG2E
                s = s_all[i * blk:(i + 1) * blk] + slope2 * neg_dist
                sink2 = sink_ref[layer, hq] * LOG2E
                m = jnp.maximum(jnp.max(s, axis=-1, keepdims=True), sink2)
                e = jnp.exp2(s - m)
                den = jnp.sum(e, axis=-1, keepdims=True) + jnp.exp2(sink2 - m)
                probs.append(e.astype(BF16))
                inv_den.append(1.0 / den)
            p = jnp.concatenate(probs, axis=0)
            o = jnp.dot(p, v2, preferred_element_type=F32)
            outs.append(jnp.concatenate(
                [o[i * blk:(i + 1) * blk] * inv_den[i] for i in range(SWA_GROUP)], axis=0))
        merged = jnp.where(low, outs[0], outs[1])
        for i in range(SWA_GROUP):
            o_ref[:, (4 * t + i) * LANES:(4 * t + i + 1) * LANES] = merged[i * blk:(i + 1) * blk]


def _swa(sinks, qa, ka, va, pos_col, pos_row, layer, *, B, S):
    T = qa.shape[0]
    blk = SWA_BLOCK
    nb = S // blk
    nq = SWA_Q_HEADS * SWA_HEAD_DIM
    nkv = SWA_KV_HEADS * SWA_HEAD_DIM
    cur = lambda b, n: (b * nb + n, 0)
    prev = lambda b, n: (b * nb + jnp.maximum(n - 1, 0), 0)
    cur3 = lambda b, n: (b * nb + n, 0, 0)
    prev3 = lambda b, n: (b * nb + jnp.maximum(n - 1, 0), 0, 0)
    return pl.pallas_call(
        functools.partial(_swa_kernel, layer=layer),
        grid=(B, nb),
        in_specs=[
            pl.BlockSpec(memory_space=pltpu.SMEM),
            pl.BlockSpec((blk, nq), cur),
            pl.BlockSpec((blk, nkv), cur),
            pl.BlockSpec((blk, nkv), prev),
            pl.BlockSpec((blk, nkv), cur),
            pl.BlockSpec((blk, nkv), prev),
            pl.BlockSpec((blk, 1), cur),
            pl.BlockSpec((1, 1, blk), cur3),
            pl.BlockSpec((1, 1, blk), prev3),
        ],
        out_specs=pl.BlockSpec((blk, nq), cur),
        out_shape=jax.ShapeDtypeStruct((T, nq), F32),
        compiler_params=_params(("parallel", "arbitrary")),
        name="swa_attn",
    )(sinks, qa, ka, ka, va, va, pos_col, pos_row, pos_row)


MLA_TILE = 1024
MLA_SUB = 256
MLA_HPS = 2


def _softmax_chain(s_ref, rows, nkeys, diag, m_ref, l_ref, acc_ref):
    blocks = [s_ref[rows, b * LANES:(b + 1) * LANES] for b in range(nkeys // LANES)]
    if diag:
        row = lax.broadcasted_iota(jnp.int32, (MLA_SUB, LANES), 0)
        lane = lax.broadcasted_iota(jnp.int32, (MLA_SUB, LANES), 1)
        nd = MLA_SUB // LANES
        for d in range(nd):
            b = len(blocks) - nd + d
            blocks[b] = jnp.where(lane + d * LANES <= row, blocks[b], -jnp.inf)
    mx = functools.reduce(jnp.maximum, blocks)
    m_prev = m_ref[rows, :]
    m_new = jnp.maximum(m_prev, jnp.max(mx, axis=-1, keepdims=True))
    alpha = jnp.exp2(m_prev - m_new)
    ps = [jnp.exp2(b - m_new) for b in blocks]
    l_ref[rows, :] = alpha * l_ref[rows, :] + functools.reduce(jnp.add, ps)
    acc_ref[rows, :] = alpha * acc_ref[rows, :]
    m_ref[rows, :] = m_new
    return jnp.concatenate([pb.astype(BF16) for pb in ps], axis=-1)


def _mla_kernel(q_ref, k_ref, v_ref, o_ref, s_ref, m_ref, l_ref, acc_ref):
    i = pl.program_id(2)
    heads = range(MLA_HPS)
    nchain = MLA_TILE // MLA_SUB
    rows = [slice(c * MLA_SUB, (c + 1) * MLA_SUB) for c in range(nchain)]
    m_ref[...] = jnp.full(m_ref.shape, -jnp.inf, F32)
    l_ref[...] = jnp.zeros(l_ref.shape, F32)
    acc_ref[...] = jnp.zeros(acc_ref.shape, F32)

    def chunk_rows(chunk):
        return pl.ds(pl.multiple_of(chunk * MLA_TILE, MLA_TILE), MLA_TILE)

    def scores(h, chunk, buf):
        k = k_ref[0, h, chunk_rows(chunk), :]
        s_ref[h, buf] = lax.dot_general(q_ref[0, h], k, NT_DIMS, preferred_element_type=F32)

    def state(h):
        return m_ref.at[h], l_ref.at[h], acc_ref.at[h]

    def full_chunk(h, chunk, buf):
        p = jnp.concatenate(
            [_softmax_chain(s_ref.at[h, buf], rows[c], MLA_TILE, False, *state(h)) for c in range(nchain)], axis=0)
        acc_ref[h] += jnp.dot(p, v_ref[0, h, chunk_rows(chunk), :], preferred_element_type=F32)

    def diag_chunk(h, buf):
        base = pl.multiple_of(i * MLA_TILE, MLA_TILE)
        for c in range(nchain):
            nkeys = (c + 1) * MLA_SUB
            p = _softmax_chain(s_ref.at[h, buf], rows[c], nkeys, True, *state(h))
            acc_ref[h, rows[c], :] += jnp.dot(p, v_ref[0, h, pl.ds(base, nkeys), :], preferred_element_type=F32)

    for h in heads:
        scores(h, 0, 0)

    def pair(jj, carry):
        for h in heads:
            scores(h, 2 * jj + 1, 1)
        for h in heads:
            full_chunk(h, 2 * jj, 0)
        for h in heads:
            scores(h, 2 * jj + 2, 0)
        for h in heads:
            full_chunk(h, 2 * jj + 1, 1)
        return carry

    lax.fori_loop(0, i // 2, pair, 0)

    @pl.when(i % 2 == 0)
    def _():
        for h in heads:
            diag_chunk(h, 0)

    @pl.when(i % 2 == 1)
    def _():
        for h in heads:
            scores(h, i, 1)
        for h in heads:
            full_chunk(h, i - 1, 0)
        for h in heads:
            diag_chunk(h, 1)

    for h in heads:
        l = jnp.sum(l_ref[h], axis=-1, keepdims=True)
        o_ref[0, :, h * MLA_V:(h + 1) * MLA_V] = acc_ref[h] / l


def _mla(qb, kb, vb):
    B, H, S, _ = qb.shape
    assert S % MLA_TILE == 0 and MLA_TILE % MLA_SUB == 0 and H % MLA_HPS == 0
    hp = MLA_HPS
    return pl.pallas_call(
        _mla_kernel,
        grid=(B, H // hp, S // MLA_TILE),
        in_specs=[
            pl.BlockSpec((1, hp, MLA_TILE, MLA_QK_PAD), lambda b, g, i: (b, g, i, 0)),
            pl.BlockSpec((1, hp, S, MLA_QK_PAD), lambda b, g, i: (b, g, 0, 0)),
            pl.BlockSpec((1, hp, S, MLA_V), lambda b, g, i: (b, g, 0, 0)),
        ],
        out_specs=pl.BlockSpec((1, MLA_TILE, hp * MLA_V), lambda b, g, i: (b, i, g)),
        out_shape=jax.ShapeDtypeStruct((B, S, H * MLA_V), F32),
        scratch_shapes=[pltpu.VMEM((hp, 2, MLA_TILE, MLA_TILE), F32),
                        pltpu.VMEM((hp, MLA_TILE, LANES), F32), pltpu.VMEM((hp, MLA_TILE, LANES), F32),
                        pltpu.VMEM((hp, MLA_TILE, MLA_V), F32)],
        compiler_params=_params(("parallel", "parallel", "arbitrary")),
        name="mla_attn",
    )(qb, kb, vb)


def _outproj_kernel(h_ref, oa_ref, ob_ref, ga_ref, gb_ref, woa_ref, wob_ref, o_ref):
    a = _rms(oa_ref[...], ga_ref[...]).astype(BF16)
    b = _rms(ob_ref[...], gb_ref[...]).astype(BF16)
    o_ref[...] = (h_ref[...] + jnp.dot(a, woa_ref[...], preferred_element_type=F32)
                  + jnp.dot(b, wob_ref[...], preferred_element_type=F32))


def _outproj(h, oa, ob, ga, gb, woa, wob, layer, *, tm=512):
    T, D = h.shape
    row = lambda w: pl.BlockSpec((tm, w), lambda i: (i, 0))
    full = lambda a: _layer_spec(a, layer)
    return pl.pallas_call(
        _outproj_kernel,
        grid=(T // tm,),
        in_specs=[row(D), row(oa.shape[1]), row(ob.shape[1]), full(ga), full(gb), full(woa), full(wob)],
        out_specs=row(D),
        out_shape=jax.ShapeDtypeStruct((T, D), F32),
        compiler_params=_params(("parallel",)),
        name="outproj",
    )(h, oa, ob, ga, gb, woa, wob)


def _ple_kernel(h_ref, p_ref, gg_ref, wgate_ref, wp_ref, gp_ref, o_ref):
    h = h_ref[...]
    pe = _rms(jnp.dot(p_ref[...].astype(BF16), wp_ref[...], preferred_element_type=F32), gp_ref[...])
    n = _rms(h, gg_ref[...]).astype(BF16)
    half = wgate_ref.shape[1] // 2
    for c in range(2):
        cols = slice(c * half, (c + 1) * half)
        gl = jnp.dot(n, wgate_ref[:, cols], preferred_element_type=F32)
        o_ref[:, cols] = h[:, cols] + pe[:, cols] / (1.0 + jnp.exp(-gl))


def _ple(h, p, gg, wgate, wp, gp, layer, *, tm=512):
    T, D = h.shape
    row = lambda w: pl.BlockSpec((tm, w), lambda i: (i, 0))
    full = lambda a: _layer_spec(a, layer)
    return pl.pallas_call(
        _ple_kernel,
        grid=(T // tm,),
        in_specs=[row(D), pl.BlockSpec((None, tm, p.shape[2]), lambda i: (layer, i, 0)),
                  full(gg), full(wgate), full(wp), full(gp)],
        out_specs=row(D),
        out_shape=jax.ShapeDtypeStruct((T, D), F32),
        compiler_params=_params(("parallel",)),
        name="ple",
    )(h, p, gg, wgate, wp, gp)


def _pad_rope(a):
    half = MLA_ROPE // 2
    zeros = jnp.zeros(a.shape[:-1] + (LANES // 2 - half,), a.dtype)
    return jnp.concatenate([a[..., :half], zeros, a[..., half:], zeros], axis=-1)


def _pair_heads(a, axis):
    heads = [8 * t + 4 * half + i for t in range(SWA_KV_HEADS // 2) for i in range(SWA_GROUP) for half in range(2)]
    return jnp.concatenate(
        [lax.slice_in_dim(a, SWA_HEAD_DIM * h, SWA_HEAD_DIM * (h + 1), axis=axis) for h in heads], axis=axis)


def kernel(x, p, positions, ffn1_norm, ffn1_w_gate, ffn1_w_up, ffn1_w_down, mix_norm, w_in, swa_q_norm, swa_k_norm, swa_sinks, mla_q_lora_norm, mla_w_uq, mla_kv_lora_norm, mla_w_ukv, mla_q_norm, mla_k_norm, out_norm_swa, out_norm_mla, w_o, ffn2_norm, ffn2_w_gate, ffn2_w_up, ffn2_w_down, ple_proj, ple_proj_norm, ple_gate_norm, ple_gate):
    B, S, D = x.shape
    depth = p.shape[0]
    T = B * S
    nq = SWA_Q_HEADS * SWA_HEAD_DIM
    nkv = SWA_KV_HEADS * SWA_HEAD_DIM
    o_kr = nq + 2 * nkv + MLA_Q_RANK + MLA_KV_RANK

    w1g, w1u, w1d = ffn1_w_gate, ffn1_w_up, ffn1_w_down
    w2g, w2u, w2d = ffn2_w_gate, ffn2_w_up, ffn2_w_down
    win = jnp.concatenate(
        [_pair_heads(w_in[..., :nq], -1), w_in[..., nq:o_kr], _pad_rope(w_in[..., o_kr:])], axis=-1).astype(BF16)
    wuq = jnp.concatenate(
        [mla_w_uq[..., MLA_QK * h:MLA_QK * h + MLA_NOPE] for h in range(MLA_HEADS)]
        + [_pad_rope(mla_w_uq[..., MLA_QK * h + MLA_NOPE:MLA_QK * (h + 1)]) for h in range(MLA_HEADS)],
        axis=-1).astype(BF16)
    wukv = mla_w_ukv.astype(BF16)
    woa = _pair_heads(w_o[:, :nq], 1).astype(BF16)
    wob = w_o[:, nq:].astype(BF16)
    wgate = ple_gate.astype(BF16)
    wp = ple_proj.astype(BF16)

    row = lambda v: v.reshape(depth, 1, -1)
    gqa = row(jnp.tile(swa_q_norm, (1, 2)))
    gka = row(jnp.tile(swa_k_norm, (1, 2)))
    gqn, gqr = row(mla_q_norm[:, :MLA_NOPE]), row(_pad_rope(mla_q_norm[:, MLA_NOPE:]))
    gkn, gkr = row(mla_k_norm[:, :MLA_NOPE]), row(_pad_rope(mla_k_norm[:, MLA_NOPE:]))
    ga, gb = row(_pair_heads(out_norm_swa, 1)), row(out_norm_mla)
    g1, g2, gmix = row(ffn1_norm), row(ffn2_norm), row(mix_norm)
    gcq, gckv = row(mla_q_lora_norm), row(mla_kv_lora_norm)
    gpg, gpp = row(ple_gate_norm), row(ple_proj_norm)

    half = MLA_ROPE // 2
    inv_freq = ROPE_THETA ** (-jnp.arange(half, dtype=F32) / half)
    inv_full = _pad_rope(jnp.concatenate([inv_freq, inv_freq])).reshape(1, LANES)
    sign_full = _pad_rope(jnp.concatenate([-jnp.ones((half,), F32), jnp.ones((half,), F32)])).reshape(1, LANES)
    pos_col = positions.reshape(T, 1)
    pos_row = positions.reshape(T // SWA_BLOCK, 1, SWA_BLOCK)
    cos_t, sin_t = _rope_tables(pos_col, inv_full, sign_full)

    pf = p.reshape(depth, T, -1)
    xf = x.reshape(T, D)
    for i in range(depth):
        h = _ffn(xf, g1, w1g, w1u, w1d, i)
        qa, ka, va, qb, kb, vb = _inproj(h, gmix, win, gqa, gka, gcq, wuq, gckv, wukv, gqn, gqr, gkn, gkr,
                                         cos_t, sin_t, i, B=B, S=S)
        oa = _swa(swa_sinks, qa, ka, va, pos_col, pos_row, i, B=B, S=S)
        ob = _mla(qb, kb, vb).reshape(T, MLA_HEADS * MLA_V)
        h = _outproj(h, oa, ob, ga, gb, woa, wob, i)
        h = _ffn(h, g2, w2g, w2u, w2d, i)
        xf = _ple(h, pf, gpg, wgate, wp, gpp, i)
    return xf.reshape(B, S, D)
```

```python
import functools
import math

import jax
import jax.numpy as jnp
from jax import lax
from jax.experimental import pallas as pl
from jax.experimental.pallas import tpu as pltpu

F32 = jnp.float32
BF16 = jnp.bfloat16

EPS = 1e-6
ROPE_THETA = 10000.0
LANES = 128
SWA_BLOCK = 128
SWA_HEAD_DIM = 64
SWA_Q_HEADS = 16
SWA_KV_HEADS = 4
SWA_GROUP = 4
MLA_HEADS = 8
MLA_NOPE = 128
MLA_ROPE = 64
MLA_QK = MLA_NOPE + MLA_ROPE
MLA_V = 128
MLA_Q_RANK = 512
MLA_KV_RANK = 256
MLA_QK_PAD = 2 * LANES
LOG2E = math.log2(math.e)
MLA_Q_PRESCALE = (MLA_QK ** -0.5) * LOG2E
SWA_Q_PRESCALE = (SWA_HEAD_DIM ** -0.5) * LOG2E
VMEM_LIMIT = 56 * 1024 * 1024
FFN_VMEM_LIMIT = 60 * 1024 * 1024

NT_DIMS = (((1,), (1,)), ((), ()))


def _rms(x, g):
    ms = jnp.mean(x * x, axis=-1, keepdims=True)
    return x * lax.rsqrt(ms + EPS) * g


def _params(sem, vmem=VMEM_LIMIT):
    return pltpu.CompilerParams(dimension_semantics=sem, vmem_limit_bytes=vmem)


def _layer_spec(a, layer):
    nd = a.ndim - 1
    return pl.BlockSpec((None,) + a.shape[1:], lambda *_: (layer,) + (0,) * nd, pipeline_mode=pl.Buffered(1))


def _ffn_kernel(x_hbm, g_ref, wg_ref, wu_ref, wd_ref, o_ref, xbuf_ref, xn_ref, x_sem, *, tm):
    i = pl.program_id(0)
    f = pl.program_id(1)

    def x_copy(tile):
        rows = pl.ds(pl.multiple_of(tile * tm, tm), tm)
        return pltpu.make_async_copy(x_hbm.at[rows, :], xbuf_ref, x_sem)

    @pl.when(jnp.logical_and(i == 0, f == 0))
    def _():
        x_copy(0).start()

    def swiglu(xn):
        gate = jnp.dot(xn, wg_ref[...].astype(BF16), preferred_element_type=F32)
        up = jnp.dot(xn, wu_ref[...].astype(BF16), preferred_element_type=F32)
        h = (0.5 * gate / (1.0 + jnp.exp(-gate)) * up).astype(BF16)
        return jnp.dot(h, wd_ref[...].astype(BF16), preferred_element_type=F32)

    @pl.when(f == 0)
    def _():
        x_copy(i).wait()
        x = xbuf_ref[...]
        xn = _rms(x, g_ref[...]).astype(BF16)
        xn_ref[...] = xn
        o_ref[...] = x + swiglu(xn)

    @pl.when(jnp.logical_and(f == 1, i + 1 < pl.num_programs(0)))
    def _():
        x_copy(i + 1).start()

    @pl.when(f > 0)
    def _():
        o_ref[...] += swiglu(xn_ref[...])


def _ffn(x, g, wg, wu, wd, layer, *, tm=1024, tf=512):
    T, D = x.shape
    Fd = wg.shape[2]
    assert T % tm == 0 and Fd % tf == 0 and Fd // tf >= 2
    return pl.pallas_call(
        functools.partial(_ffn_kernel, tm=tm),
        grid=(T // tm, Fd // tf),
        in_specs=[
            pl.BlockSpec(memory_space=pl.ANY),
            _layer_spec(g, layer),
            pl.BlockSpec((None, D, tf), lambda i, f: (layer, 0, f)),
            pl.BlockSpec((None, D, tf), lambda i, f: (layer, 0, f)),
            pl.BlockSpec((None, tf, D), lambda i, f: (layer, f, 0)),
        ],
        out_specs=pl.BlockSpec((tm, D), lambda i, f: (i, 0)),
        out_shape=jax.ShapeDtypeStruct((T, D), F32),
        scratch_shapes=[pltpu.VMEM((tm, D), F32), pltpu.VMEM((tm, D), BF16), pltpu.SemaphoreType.DMA(())],
        compiler_params=_params(("arbitrary", "arbitrary"), FFN_VMEM_LIMIT),
        name="ffn",
    )(x, g, wg, wu, wd)


def _rope_table_kernel(pos_ref, inv_ref, sign_ref, cos_ref, sin_ref):
    ang = pos_ref[...].astype(F32) * inv_ref[...]
    cos_ref[...] = jnp.cos(ang)
    sin_ref[...] = jnp.sin(ang) * sign_ref[...]


def _rope_tables(pos_col, inv_full, sign_full, *, tm=2048):
    T = pos_col.shape[0]
    return pl.pallas_call(
        _rope_table_kernel,
        grid=(T // tm,),
        in_specs=[
            pl.BlockSpec((tm, 1), lambda i: (i, 0)),
            pl.BlockSpec((1, LANES), lambda i: (0, 0)),
            pl.BlockSpec((1, LANES), lambda i: (0, 0)),
        ],
        out_specs=[pl.BlockSpec((tm, LANES), lambda i: (i, 0))] * 2,
        out_shape=[jax.ShapeDtypeStruct((T, LANES), F32)] * 2,
        compiler_params=_params(("parallel",)),
        name="rope_tables",
    )(pos_col, inv_full, sign_full)


def _inproj_kernel(x_ref, gmix_ref, win_ref, gqa_ref, gka_ref, gcq_ref, wuq_ref, gckv_ref, wukv_ref,
                   gqn_ref, gqr_ref, gkn_ref, gkr_ref, cos_ref, sin_ref,
                   qa_ref, ka_ref, va_ref, qb_ref, kb_ref, vb_ref):
    n = _rms(x_ref[...], gmix_ref[...]).astype(BF16)
    o_lat = SWA_Q_HEADS * SWA_HEAD_DIM + 2 * SWA_KV_HEADS * SWA_HEAD_DIM
    z_lat = jnp.dot(n, win_ref[:, o_lat:], preferred_element_type=F32)
    z_swa = jnp.dot(n, win_ref[:, :o_lat], preferred_element_type=F32)
    z = jnp.concatenate([z_swa, z_lat], axis=-1)

    low = lax.broadcasted_iota(jnp.int32, (1, LANES), 1) < SWA_HEAD_DIM

    def head_pair_norm(blk, g):
        sq = blk * blk
        s_lo = jnp.sum(jnp.where(low, sq, 0.0), axis=-1, keepdims=True)
        s_hi = jnp.sum(jnp.where(low, 0.0, sq), axis=-1, keepdims=True)
        r = jnp.where(low, lax.rsqrt(s_lo / SWA_HEAD_DIM + EPS), lax.rsqrt(s_hi / SWA_HEAD_DIM + EPS))
        return blk * r * g

    nq = SWA_Q_HEADS * SWA_HEAD_DIM
    nkv = SWA_KV_HEADS * SWA_HEAD_DIM
    for c in range(nq // LANES):
        qa_ref[:, c * LANES:(c + 1) * LANES] = (head_pair_norm(
            z[:, c * LANES:(c + 1) * LANES], gqa_ref[...]) * SWA_Q_PRESCALE).astype(BF16)
    for c in range(nkv // LANES):
        ka_ref[:, c * LANES:(c + 1) * LANES] = head_pair_norm(
            z[:, nq + c * LANES:nq + (c + 1) * LANES], gka_ref[...]).astype(BF16)
    va_ref[...] = z[:, nq + nkv:nq + 2 * nkv].astype(BF16)

    o_cq = nq + 2 * nkv
    o_ckv = o_cq + MLA_Q_RANK
    o_kr = o_ckv + MLA_KV_RANK
    cq = _rms(z[:, o_cq:o_ckv], gcq_ref[...]).astype(BF16)
    qb = jnp.dot(cq, wuq_ref[...], preferred_element_type=F32)
    ckv = _rms(z[:, o_ckv:o_kr], gckv_ref[...]).astype(BF16)
    kv = jnp.dot(ckv, wukv_ref[...], preferred_element_type=F32)
    kr = z[:, o_kr:o_kr + LANES]

    cos = cos_ref[...]
    sin = sin_ref[...]

    def rope(t):
        return t * cos + pltpu.roll(t, LANES // 2, 1) * sin

    kr_rot = rope(kr * gkr_ref[...])
    ssq_kr = jnp.sum(kr * kr, axis=-1, keepdims=True)
    for h in range(MLA_HEADS):
        qn = qb[:, h * LANES:(h + 1) * LANES]
        qr = qb[:, (MLA_HEADS + h) * LANES:(MLA_HEADS + h + 1) * LANES]
        ssq = jnp.sum(qn * qn, axis=-1, keepdims=True) + jnp.sum(qr * qr, axis=-1, keepdims=True)
        r = lax.rsqrt(ssq / MLA_QK + EPS)
        qb_ref[0, h, :, 0:LANES] = (qn * r * gqn_ref[...] * MLA_Q_PRESCALE).astype(BF16)
        qb_ref[0, h, :, LANES:2 * LANES] = (rope(qr * r * gqr_ref[...]) * MLA_Q_PRESCALE).astype(BF16)
        kn = kv[:, 2 * h * LANES:(2 * h + 1) * LANES]
        vh = kv[:, (2 * h + 1) * LANES:(2 * h + 2) * LANES]
        rk = lax.rsqrt((jnp.sum(kn * kn, axis=-1, keepdims=True) + ssq_kr) / MLA_QK + EPS)
        kb_ref[0, h, :, 0:LANES] = (kn * rk * gkn_ref[...]).astype(BF16)
        kb_ref[0, h, :, LANES:2 * LANES] = (kr_rot * rk).astype(BF16)
        vb_ref[0, h] = vh.astype(BF16)


def _inproj(x, gmix, win, gqa, gka, gcq, wuq, gckv, wukv, gqn, gqr, gkn, gkr, cos, sin, layer, *, B, S, tm=512):
    T, D = x.shape
    ns = S // tm
    nq = SWA_Q_HEADS * SWA_HEAD_DIM
    nkv = SWA_KV_HEADS * SWA_HEAD_DIM
    full = lambda a: _layer_spec(a, layer)

    row = lambda w: pl.BlockSpec((tm, w), lambda i: (i, 0))
    head = lambda w: pl.BlockSpec((1, MLA_HEADS, tm, w), lambda i: (i // ns, 0, i % ns, 0))
    return pl.pallas_call(
        _inproj_kernel,
        grid=(T // tm,),
        in_specs=[row(D), full(gmix), full(win), full(gqa), full(gka), full(gcq), full(wuq), full(gckv),
                  full(wukv), full(gqn), full(gqr), full(gkn), full(gkr), row(LANES), row(LANES)],
        out_specs=[row(nq), row(nkv), row(nkv), head(MLA_QK_PAD), head(MLA_QK_PAD), head(MLA_V)],
        out_shape=[
            jax.ShapeDtypeStruct((T, nq), BF16),
            jax.ShapeDtypeStruct((T, nkv), BF16),
            jax.ShapeDtypeStruct((T, nkv), BF16),
            jax.ShapeDtypeStruct((B, MLA_HEADS, S, MLA_QK_PAD), BF16),
            jax.ShapeDtypeStruct((B, MLA_HEADS, S, MLA_QK_PAD), BF16),
            jax.ShapeDtypeStruct((B, MLA_HEADS, S, MLA_V), BF16),
        ],
        compiler_params=_params(("parallel",)),
        name="inproj",
    )(x, gmix, win, gqa, gka, gcq, wuq, gckv, wukv, gqn, gqr, gkn, gkr, cos, sin)


def _swa_kernel(sink_ref, q_ref, kc_ref, kp_ref, vc_ref, vp_ref, pq_ref, pkc_ref, pkp_ref, o_ref, *, layer):
    n = pl.program_id(1)
    blk = SWA_BLOCK
    pq = pq_ref[...]
    pk = jnp.concatenate([pkp_ref[0], pkc_ref[0]], axis=1)
    qi = lax.broadcasted_iota(jnp.int32, (blk, 2 * blk), 0)
    kj = lax.broadcasted_iota(jnp.int32, (blk, 2 * blk), 1)
    diff = qi + blk - kj
    valid = (diff >= 0) & (diff < blk) & ((n > 0) | (kj >= blk))
    neg_dist = jnp.where(valid, -jnp.abs(pq - pk).astype(F32), -jnp.inf)
    low = lax.broadcasted_iota(jnp.int32, (1, LANES), 1) < SWA_HEAD_DIM

    for t in range(SWA_KV_HEADS // 2):
        cols = slice(t * LANES, (t + 1) * LANES)
        k2 = jnp.concatenate([kp_ref[:, cols], kc_ref[:, cols]], axis=0)
        v2 = jnp.concatenate([vp_ref[:, cols], vc_ref[:, cols]], axis=0)
        qs = jnp.concatenate(
            [q_ref[:, (4 * t + i) * LANES:(4 * t + i + 1) * LANES] for i in range(SWA_GROUP)], axis=0)
        outs = []
        for half in range(2):
            g = 2 * t + half
            keep = low if half == 0 else jnp.logical_not(low)
            qm = jnp.where(keep, qs, jnp.zeros_like(qs))
            s_all = lax.dot_general(qm, k2, NT_DIMS, preferred_element_type=F32)
            probs, inv_den = [], []
            for i in range(SWA_GROUP):
                hq = SWA_GROUP * g + i
                slope2 = float(2.0 ** (-8.0 * (hq + 1) / SWA_Q_HEADS)) * LOG2E
                s = s_all[i * blk:(i + 1) * blk] + slope2 * neg_dist
                sink2 = sink_ref[layer, hq] * LOG2E
                m = jnp.maximum(jnp.max(s, axis=-1, keepdims=True), sink2)
                e = jnp.exp2(s - m)
                den = jnp.sum(e, axis=-1, keepdims=True) + jnp.exp2(sink2 - m)
                probs.append(e.astype(BF16))
                inv_den.append(1.0 / den)
            p = jnp.concatenate(probs, axis=0)
            o = jnp.dot(p, v2, preferred_element_type=F32)
            outs.append(jnp.concatenate(
                [o[i * blk:(i + 1) * blk] * inv_den[i] for i in range(SWA_GROUP)], axis=0))
        merged = jnp.where(low, outs[0], outs[1])
        for i in range(SWA_GROUP):
            o_ref[:, (4 * t + i) * LANES:(4 * t + i + 1) * LANES] = merged[i * blk:(i + 1) * blk]


def _swa(sinks, qa, ka, va, pos_col, pos_row, layer, *, B, S):
    T = qa.shape[0]
    blk = SWA_BLOCK
    nb = S // blk
    nq = SWA_Q_HEADS * SWA_HEAD_DIM
    nkv = SWA_KV_HEADS * SWA_HEAD_DIM
    cur = lambda b, n: (b * nb + n, 0)
    prev = lambda b, n: (b * nb + jnp.maximum(n - 1, 0), 0)
    cur3 = lambda b, n: (b * nb + n, 0, 0)
    prev3 = lambda b, n: (b * nb + jnp.maximum(n - 1, 0), 0, 0)
    return pl.pallas_call(
        functools.partial(_swa_kernel, layer=layer),
        grid=(B, nb),
        in_specs=[
            pl.BlockSpec(memory_space=pltpu.SMEM),
            pl.BlockSpec((blk, nq), cur),
            pl.BlockSpec((blk, nkv), cur),
            pl.BlockSpec((blk, nkv), prev),
            pl.BlockSpec((blk, nkv), cur),
            pl.BlockSpec((blk, nkv), prev),
            pl.BlockSpec((blk, 1), cur),
            pl.BlockSpec((1, 1, blk), cur3),
            pl.BlockSpec((1, 1, blk), prev3),
        ],
        out_specs=pl.BlockSpec((blk, nq), cur),
        out_shape=jax.ShapeDtypeStruct((T, nq), F32),
        compiler_params=_params(("parallel", "arbitrary")),
        name="swa_attn",
    )(sinks, qa, ka, ka, va, va, pos_col, pos_row, pos_row)


MLA_TILE = 1024
MLA_SUB = 256
MLA_HPS = 2


def _softmax_chain(s_ref, rows, nkeys, diag, m_ref, l_ref, acc_ref):
    blocks = [s_ref[rows, b * LANES:(b + 1) * LANES] for b in range(nkeys // LANES)]
    if diag:
        row = lax.broadcasted_iota(jnp.int32, (MLA_SUB, LANES), 0)
        lane = lax.broadcasted_iota(jnp.int32, (MLA_SUB, LANES), 1)
        nd = MLA_SUB // LANES
        for d in range(nd):
            b = len(blocks) - nd + d
            blocks[b] = jnp.where(lane + d * LANES <= row, blocks[b], -jnp.inf)
    mx = functools.reduce(jnp.maximum, blocks)
    m_prev = m_ref[rows, :]
    m_new = jnp.maximum(m_prev, jnp.max(mx, axis=-1, keepdims=True))
    alpha = jnp.exp2(m_prev - m_new)
    ps = [jnp.exp2(b - m_new) for b in blocks]
    l_ref[rows, :] = alpha * l_ref[rows, :] + functools.reduce(jnp.add, ps)
    acc_ref[rows, :] = alpha * acc_ref[rows, :]
    m_ref[rows, :] = m_new
    return jnp.concatenate([pb.astype(BF16) for pb in ps], axis=-1)


def _mla_kernel(q_ref, k_ref, v_ref, o_ref, s_ref, m_ref, l_ref, acc_ref):
    i = pl.program_id(2)
    heads = range(MLA_HPS)
    nchain = MLA_TILE // MLA_SUB
    rows = [slice(c * MLA_SUB, (c + 1) * MLA_SUB) for c in range(nchain)]
    m_ref[...] = jnp.full(m_ref.shape, -jnp.inf, F32)
    l_ref[...] = jnp.zeros(l_ref.shape, F32)
    acc_ref[...] = jnp.zeros(acc_ref.shape, F32)

    def chunk_rows(chunk):
        return pl.ds(pl.multiple_of(chunk * MLA_TILE, MLA_TILE), MLA_TILE)

    def scores(h, chunk, buf):
        k = k_ref[0, h, chunk_rows(chunk), :]
        s_ref[h, buf] = lax.dot_general(q_ref[0, h], k, NT_DIMS, preferred_element_type=F32)

    def state(h):
        return m_ref.at[h], l_ref.at[h], acc_ref.at[h]

    def full_chunk(h, chunk, buf):
        p = jnp.concatenate(
            [_softmax_chain(s_ref.at[h, buf], rows[c], MLA_TILE, False, *state(h)) for c in range(nchain)], axis=0)
        acc_ref[h] += jnp.dot(p, v_ref[0, h, chunk_rows(chunk), :], preferred_element_type=F32)

    def diag_chunk(h, buf):
        base = pl.multiple_of(i * MLA_TILE, MLA_TILE)
        for c in range(nchain):
            nkeys = (c + 1) * MLA_SUB
            p = _softmax_chain(s_ref.at[h, buf], rows[c], nkeys, True, *state(h))
            acc_ref[h, rows[c], :] += jnp.dot(p, v_ref[0, h, pl.ds(base, nkeys), :], preferred_element_type=F32)

    for h in heads:
        scores(h, 0, 0)

    def pair(jj, carry):
        for h in heads:
            scores(h, 2 * jj + 1, 1)
        for h in heads:
            full_chunk(h, 2 * jj, 0)
        for h in heads:
            scores(h, 2 * jj + 2, 0)
        for h in heads:
            full_chunk(h, 2 * jj + 1, 1)
        return carry

    lax.fori_loop(0, i // 2, pair, 0)

    @pl.when(i % 2 == 0)
    def _():
        for h in heads:
            diag_chunk(h, 0)

    @pl.when(i % 2 == 1)
    def _():
        for h in heads:
            scores(h, i, 1)
        for h in heads:
            full_chunk(h, i - 1, 0)
        for h in heads:
            diag_chunk(h, 1)

    for h in heads:
        l = jnp.sum(l_ref[h], axis=-1, keepdims=True)
        o_ref[0, :, h * MLA_V:(h + 1) * MLA_V] = acc_ref[h] / l


def _mla(qb, kb, vb):
    B, H, S, _ = qb.shape
    assert S % MLA_TILE == 0 and MLA_TILE % MLA_SUB == 0 and H % MLA_HPS == 0
    hp = MLA_HPS
    return pl.pallas_call(
        _mla_kernel,
        grid=(B, H // hp, S // MLA_TILE),
        in_specs=[
            pl.BlockSpec((1, hp, MLA_TILE, MLA_QK_PAD), lambda b, g, i: (b, g, i, 0)),
            pl.BlockSpec((1, hp, S, MLA_QK_PAD), lambda b, g, i: (b, g, 0, 0)),
            pl.BlockSpec((1, hp, S, MLA_V), lambda b, g, i: (b, g, 0, 0)),
        ],
        out_specs=pl.BlockSpec((1, MLA_TILE, hp * MLA_V), lambda b, g, i: (b, i, g)),
        out_shape=jax.ShapeDtypeStruct((B, S, H * MLA_V), F32),
        scratch_shapes=[pltpu.VMEM((hp, 2, MLA_TILE, MLA_TILE), F32),
                        pltpu.VMEM((hp, MLA_TILE, LANES), F32), pltpu.VMEM((hp, MLA_TILE, LANES), F32),
                        pltpu.VMEM((hp, MLA_TILE, MLA_V), F32)],
        compiler_params=_params(("parallel", "parallel", "arbitrary")),
        name="mla_attn",
    )(qb, kb, vb)


def _outproj_kernel(h_ref, oa_ref, ob_ref, ga_ref, gb_ref, woa_ref, wob_ref, o_ref):
    a = _rms(oa_ref[...], ga_ref[...]).astype(BF16)
    b = _rms(ob_ref[...], gb_ref[...]).astype(BF16)
    o_ref[...] = (h_ref[...] + jnp.dot(a, woa_ref[...], preferred_element_type=F32)
                  + jnp.dot(b, wob_ref[...], preferred_element_type=F32))


def _outproj(h, oa, ob, ga, gb, woa, wob, layer, *, tm=512):
    T, D = h.shape
    row = lambda w: pl.BlockSpec((tm, w), lambda i: (i, 0))
    full = lambda a: _layer_spec(a, layer)
    return pl.pallas_call(
        _outproj_kernel,
        grid=(T // tm,),
        in_specs=[row(D), row(oa.shape[1]), row(ob.shape[1]), full(ga), full(gb), full(woa), full(wob)],
        out_specs=row(D),
        out_shape=jax.ShapeDtypeStruct((T, D), F32),
        compiler_params=_params(("parallel",)),
        name="outproj",
    )(h, oa, ob, ga, gb, woa, wob)


def _ple_kernel(h_ref, p_ref, gg_ref, wgate_ref, wp_ref, gp_ref, o_ref):
    h = h_ref[...]
    pe = _rms(jnp.dot(p_ref[...].astype(BF16), wp_ref[...], preferred_element_type=F32), gp_ref[...])
    n = _rms(h, gg_ref[...]).astype(BF16)
    half = wgate_ref.shape[1] // 2
    for c in range(2):
        cols = slice(c * half, (c + 1) * half)
        gl = jnp.dot(n, wgate_ref[:, cols], preferred_element_type=F32)
        o_ref[:, cols] = h[:, cols] + pe[:, cols] / (1.0 + jnp.exp(-gl))


def _ple(h, p, gg, wgate, wp, gp, layer, *, tm=512):
    T, D = h.shape
    row = lambda w: pl.BlockSpec((tm, w), lambda i: (i, 0))
    full = lambda a: _layer_spec(a, layer)
    return pl.pallas_call(
        _ple_kernel,
        grid=(T // tm,),
        in_specs=[row(D), pl.BlockSpec((None, tm, p.shape[2]), lambda i: (layer, i, 0)),
                  full(gg), full(wgate), full(wp), full(gp)],
        out_specs=row(D),
        out_shape=jax.ShapeDtypeStruct((T, D), F32),
        compiler_params=_params(("parallel",)),
        name="ple",
    )(h, p, gg, wgate, wp, gp)


def _pad_rope(a):
    half = MLA_ROPE // 2
    zeros = jnp.zeros(a.shape[:-1] + (LANES // 2 - half,), a.dtype)
    return jnp.concatenate([a[..., :half], zeros, a[..., half:], zeros], axis=-1)


def _pair_heads(a, axis):
    heads = [8 * t + 4 * half + i for t in range(SWA_KV_HEADS // 2) for i in range(SWA_GROUP) for half in range(2)]
    return jnp.concatenate(
        [lax.slice_in_dim(a, SWA_HEAD_DIM * h, SWA_HEAD_DIM * (h + 1), axis=axis) for h in heads], axis=axis)


def kernel(x, p, positions, ffn1_norm, ffn1_w_gate, ffn1_w_up, ffn1_w_down, mix_norm, w_in, swa_q_norm, swa_k_norm, swa_sinks, mla_q_lora_norm, mla_w_uq, mla_kv_lora_norm, mla_w_ukv, mla_q_norm, mla_k_norm, out_norm_swa, out_norm_mla, w_o, ffn2_norm, ffn2_w_gate, ffn2_w_up, ffn2_w_down, ple_proj, ple_proj_norm, ple_gate_norm, ple_gate):
    B, S, D = x.shape
    depth = p.shape[0]
    T = B * S
    nq = SWA_Q_HEADS * SWA_HEAD_DIM
    nkv = SWA_KV_HEADS * SWA_HEAD_DIM
    o_kr = nq + 2 * nkv + MLA_Q_RANK + MLA_KV_RANK

    w1g, w1u, w1d = ffn1_w_gate, ffn1_w_up, ffn1_w_down
    w2g, w2u, w2d = ffn2_w_gate, ffn2_w_up, ffn2_w_down
    win = jnp.concatenate(
        [_pair_heads(w_in[..., :nq], -1), w_in[..., nq:o_kr], _pad_rope(w_in[..., o_kr:])], axis=-1).astype(BF16)
    wuq = jnp.concatenate(
        [mla_w_uq[..., MLA_QK * h:MLA_QK * h + MLA_NOPE] for h in range(MLA_HEADS)]
        + [_pad_rope(mla_w_uq[..., MLA_QK * h + MLA_NOPE:MLA_QK * (h + 1)]) for h in range(MLA_HEADS)],
        axis=-1).astype(BF16)
    wukv = mla_w_ukv.astype(BF16)
    woa = _pair_heads(w_o[:, :nq], 1).astype(BF16)
    wob = w_o[:, nq:].astype(BF16)
    wgate = ple_gate.astype(BF16)
    wp = ple_proj.astype(BF16)

    row = lambda v: v.reshape(depth, 1, -1)
    gqa = row(jnp.tile(swa_q_norm, (1, 2)))
    gka = row(jnp.tile(swa_k_norm, (1, 2)))
    gqn, gqr = row(mla_q_norm[:, :MLA_NOPE]), row(_pad_rope(mla_q_norm[:, MLA_NOPE:]))
    gkn, gkr = row(mla_k_norm[:, :MLA_NOPE]), row(_pad_rope(mla_k_norm[:, MLA_NOPE:]))
    ga, gb = row(_pair_heads(out_norm_swa, 1)), row(out_norm_mla)
    g1, g2, gmix = row(ffn1_norm), row(ffn2_norm), row(mix_norm)
    gcq, gckv = row(mla_q_lora_norm), row(mla_kv_lora_norm)
    gpg, gpp = row(ple_gate_norm), row(ple_proj_norm)

    half = MLA_ROPE // 2
    inv_freq = ROPE_THETA ** (-jnp.arange(half, dtype=F32) / half)
    inv_full = _pad_rope(jnp.concatenate([inv_freq, inv_freq])).reshape(1, LANES)
    sign_full = _pad_rope(jnp.concatenate([-jnp.ones((half,), F32), jnp.ones((half,), F32)])).reshape(1, LANES)
    pos_col = positions.reshape(T, 1)
    pos_row = positions.reshape(T // SWA_BLOCK, 1, SWA_BLOCK)
    cos_t, sin_t = _rope_tables(pos_col, inv_full, sign_full)

    pf = p.reshape(depth, T, -1)
    xf = x.reshape(T, D)
    for i in range(depth):
        h = _ffn(xf, g1, w1g, w1u, w1d, i)
        qa, ka, va, qb, kb, vb = _inproj(h, gmix, win, gqa, gka, gcq, wuq, gckv, wukv, gqn, gqr, gkn, gkr,
                                         cos_t, sin_t, i, B=B, S=S)
        oa = _swa(swa_sinks, qa, ka, va, pos_col, pos_row, i, B=B, S=S)
        ob = _mla(qb, kb, vb).reshape(T, MLA_HEADS * MLA_V)
        h = _outproj(h, oa, ob, ga, gb, woa, wob, i)
        h = _ffn(h, g2, w2g, w2u, w2d, i)
        xf = _ple(h, pf, gpg, wgate, wp, gpp, i)
    return xf.reshape(B, S, D)
```

```python
import functools
import math

import jax
import jax.numpy as jnp
from jax import lax
from jax.experimental import pallas as pl
from jax.experimental.pallas import tpu as pltpu

F32 = jnp.float32
BF16 = jnp.bfloat16

EPS = 1e-6
ROPE_THETA = 10000.0
LANES = 128
SWA_BLOCK = 128
SWA_HEAD_DIM = 64
SWA_Q_HEADS = 16
SWA_KV_HEADS = 4
SWA_GROUP = 4
MLA_HEADS = 8
MLA_NOPE = 128
MLA_ROPE = 64
MLA_QK = MLA_NOPE + MLA_ROPE
MLA_V = 128
MLA_Q_RANK = 512
MLA_KV_RANK = 256
MLA_QK_PAD = 2 * LANES
LOG2E = math.log2(math.e)
MLA_Q_PRESCALE = (MLA_QK ** -0.5) * LOG2E
SWA_Q_PRESCALE = (SWA_HEAD_DIM ** -0.5) * LOG2E
VMEM_LIMIT = 56 * 1024 * 1024
FFN_VMEM_LIMIT = 60 * 1024 * 1024

NT_DIMS = (((1,), (1,)), ((), ()))


def _rms(x, g):
    ms = jnp.mean(x * x, axis=-1, keepdims=True)
    return x * lax.rsqrt(ms + EPS) * g


def _params(sem, vmem=VMEM_LIMIT):
    return pltpu.CompilerParams(dimension_semantics=sem, vmem_limit_bytes=vmem)


def _layer_spec(a, layer):
    nd = a.ndim - 1
    return pl.BlockSpec((None,) + a.shape[1:], lambda *_: (layer,) + (0,) * nd, pipeline_mode=pl.Buffered(1))


def _ffn_kernel(x_hbm, g_ref, wg_ref, wu_ref, wd_ref, o_ref, xbuf_ref, xn_ref, x_sem, *, tm):
    i = pl.program_id(0)
    f = pl.program_id(1)

    def x_copy(tile):
        rows = pl.ds(pl.multiple_of(tile * tm, tm), tm)
        return pltpu.make_async_copy(x_hbm.at[rows, :], xbuf_ref, x_sem)

    @pl.when(jnp.logical_and(i == 0, f == 0))
    def _():
        x_copy(0).start()

    def swiglu(xn):
        gate = jnp.dot(xn, wg_ref[...].astype(BF16), preferred_element_type=F32)
        up = jnp.dot(xn, wu_ref[...].astype(BF16), preferred_element_type=F32)
        h = (0.5 * gate / (1.0 + jnp.exp(-gate)) * up).astype(BF16)
        return jnp.dot(h, wd_ref[...].astype(BF16), preferred_element_type=F32)

    @pl.when(f == 0)
    def _():
        x_copy(i).wait()
        x = xbuf_ref[...]
        xn = _rms(x, g_ref[...]).astype(BF16)
        xn_ref[...] = xn
        o_ref[...] = x + swiglu(xn)

    @pl.when(jnp.logical_and(f == 1, i + 1 < pl.num_programs(0)))
    def _():
        x_copy(i + 1).start()

    @pl.when(f > 0)
    def _():
        o_ref[...] += swiglu(xn_ref[...])


def _ffn(x, g, wg, wu, wd, layer, *, tm=1024, tf=512):
    T, D = x.shape
    Fd = wg.shape[2]
    assert T % tm == 0 and Fd % tf == 0 and Fd // tf >= 2
    return pl.pallas_call(
        functools.partial(_ffn_kernel, tm=tm),
        grid=(T // tm, Fd // tf),
        in_specs=[
            pl.BlockSpec(memory_space=pl.ANY),
            _layer_spec(g, layer),
            pl.BlockSpec((None, D, tf), lambda i, f: (layer, 0, f)),
            pl.BlockSpec((None, D, tf), lambda i, f: (layer, 0, f)),
            pl.BlockSpec((None, tf, D), lambda i, f: (layer, f, 0)),
        ],
        out_specs=pl.BlockSpec((tm, D), lambda i, f: (i, 0)),
        out_shape=jax.ShapeDtypeStruct((T, D), F32),
        scratch_shapes=[pltpu.VMEM((tm, D), F32), pltpu.VMEM((tm, D), BF16), pltpu.SemaphoreType.DMA(())],
        compiler_params=_params(("arbitrary", "arbitrary"), FFN_VMEM_LIMIT),
        name="ffn",
    )(x, g, wg, wu, wd)


def _rope_table_kernel(pos_ref, inv_ref, sign_ref, cos_ref, sin_ref):
    ang = pos_ref[...].astype(F32) * inv_ref[...]
    cos_ref[...] = jnp.cos(ang)
    sin_ref[...] = jnp.sin(ang) * sign_ref[...]


def _rope_tables(pos_col, inv_full, sign_full, *, tm=2048):
    T = pos_col.shape[0]
    return pl.pallas_call(
        _rope_table_kernel,
        grid=(T // tm,),
        in_specs=[
            pl.BlockSpec((tm, 1), lambda i: (i, 0)),
            pl.BlockSpec((1, LANES), lambda i: (0, 0)),
            pl.BlockSpec((1, LANES), lambda i: (0, 0)),
        ],
        out_specs=[pl.BlockSpec((tm, LANES), lambda i: (i, 0))] * 2,
        out_shape=[jax.ShapeDtypeStruct((T, LANES), F32)] * 2,
        compiler_params=_params(("parallel",)),
        name="rope_tables",
    )(pos_col, inv_full, sign_full)


def _inproj_kernel(x_ref, gmix_ref, win_ref, gqa_ref, gka_ref, gcq_ref, wuq_ref, gckv_ref, wukv_ref,
                   gqn_ref, gqr_ref, gkn_ref, gkr_ref, cos_ref, sin_ref,
                   qa_ref, ka_ref, va_ref, qb_ref, kb_ref, vb_ref):
    n = _rms(x_ref[...], gmix_ref[...]).astype(BF16)
    o_lat = SWA_Q_HEADS * SWA_HEAD_DIM + 2 * SWA_KV_HEADS * SWA_HEAD_DIM
    z_lat = jnp.dot(n, win_ref[:, o_lat:], preferred_element_type=F32)
    z_swa = jnp.dot(n, win_ref[:, :o_lat], preferred_element_type=F32)
    z = jnp.concatenate([z_swa, z_lat], axis=-1)

    low = lax.broadcasted_iota(jnp.int32, (1, LANES), 1) < SWA_HEAD_DIM

    def head_pair_norm(blk, g):
        sq = blk * blk
        s_lo = jnp.sum(jnp.where(low, sq, 0.0), axis=-1, keepdims=True)
        s_hi = jnp.sum(jnp.where(low, 0.0, sq), axis=-1, keepdims=True)
        r = jnp.where(low, lax.rsqrt(s_lo / SWA_HEAD_DIM + EPS), lax.rsqrt(s_hi / SWA_HEAD_DIM + EPS))
        return blk * r * g

    nq = SWA_Q_HEADS * SWA_HEAD_DIM
    nkv = SWA_KV_HEADS * SWA_HEAD_DIM
    for c in range(nq // LANES):
        qa_ref[:, c * LANES:(c + 1) * LANES] = (head_pair_norm(
            z[:, c * LANES:(c + 1) * LANES], gqa_ref[...]) * SWA_Q_PRESCALE).astype(BF16)
    for c in range(nkv // LANES):
        ka_ref[:, c * LANES:(c + 1) * LANES] = head_pair_norm(
            z[:, nq + c * LANES:nq + (c + 1) * LANES], gka_ref[...]).astype(BF16)
    va_ref[...] = z[:, nq + nkv:nq + 2 * nkv].astype(BF16)

    o_cq = nq + 2 * nkv
    o_ckv = o_cq + MLA_Q_RANK
    o_kr = o_ckv + MLA_KV_RANK
    cq = _rms(z[:, o_cq:o_ckv], gcq_ref[...]).astype(BF16)
    qb = jnp.dot(cq, wuq_ref[...], preferred_element_type=F32)
    ckv = _rms(z[:, o_ckv:o_kr], gckv_ref[...]).astype(BF16)
    kv = jnp.dot(ckv, wukv_ref[...], preferred_element_type=F32)
    kr = z[:, o_kr:o_kr + LANES]

    cos = cos_ref[...]
    sin = sin_ref[...]

    def rope(t):
        return t * cos + pltpu.roll(t, LANES // 2, 1) * sin

    kr_rot = rope(kr * gkr_ref[...])
    ssq_kr = jnp.sum(kr * kr, axis=-1, keepdims=True)
    for h in range(MLA_HEADS):
        qn = qb[:, h * LANES:(h + 1) * LANES]
        qr = qb[:, (MLA_HEADS + h) * LANES:(MLA_HEADS + h + 1) * LANES]
        ssq = jnp.sum(qn * qn, axis=-1, keepdims=True) + jnp.sum(qr * qr, axis=-1, keepdims=True)
        r = lax.rsqrt(ssq / MLA_QK + EPS)
        qb_ref[0, h, :, 0:LANES] = (qn * r * gqn_ref[...] * MLA_Q_PRESCALE).astype(BF16)
        qb_ref[0, h, :, LANES:2 * LANES] = (rope(qr * r * gqr_ref[...]) * MLA_Q_PRESCALE).astype(BF16)
        kn = kv[:, 2 * h * LANES:(2 * h + 1) * LANES]
        vh = kv[:, (2 * h + 1) * LANES:(2 * h + 2) * LANES]
        rk = lax.rsqrt((jnp.sum(kn * kn, axis=-1, keepdims=True) + ssq_kr) / MLA_QK + EPS)
        kb_ref[0, h, :, 0:LANES] = (kn * rk * gkn_ref[...]).astype(BF16)
        kb_ref[0, h, :, LANES:2 * LANES] = (kr_rot * rk).astype(BF16)
        vb_ref[0, h] = vh.astype(BF16)


def _inproj(x, gmix, win, gqa, gka, gcq, wuq, gckv, wukv, gqn, gqr, gkn, gkr, cos, sin, layer, *, B, S, tm=512):
    T, D = x.shape
    ns = S // tm
    nq = SWA_Q_HEADS * SWA_HEAD_DIM
    nkv = SWA_KV_HEADS * SWA_HEAD_DIM
    full = lambda a: _layer_spec(a, layer)

    row = lambda w: pl.BlockSpec((tm, w), lambda i: (i, 0))
    head = lambda w: pl.BlockSpec((1, MLA_HEADS, tm, w), lambda i: (i // ns, 0, i % ns, 0))
    return pl.pallas_call(
        _inproj_kernel,
        grid=(T // tm,),
        in_specs=[row(D), full(gmix), full(win), full(gqa), full(gka), full(gcq), full(wuq), full(gckv),
                  full(wukv), full(gqn), full(gqr), full(gkn), full(gkr), row(LANES), row(LANES)],
        out_specs=[row(nq), row(nkv), row(nkv), head(MLA_QK_PAD), head(MLA_QK_PAD), head(MLA_V)],
        out_shape=[
            jax.ShapeDtypeStruct((T, nq), BF16),
            jax.ShapeDtypeStruct((T, nkv), BF16),
            jax.ShapeDtypeStruct((T, nkv), BF16),
            jax.ShapeDtypeStruct((B, MLA_HEADS, S, MLA_QK_PAD), BF16),
            jax.ShapeDtypeStruct((B, MLA_HEADS, S, MLA_QK_PAD), BF16),
            jax.ShapeDtypeStruct((B, MLA_HEADS, S, MLA_V), BF16),
        ],
        compiler_params=_params(("parallel",)),
        name="inproj",
    )(x, gmix, win, gqa, gka, gcq, wuq, gckv, wukv, gqn, gqr, gkn, gkr, cos, sin)


def _swa_kernel(sink_ref, q_ref, kc_ref, kp_ref, vc_ref, vp_ref, pq_ref, pkc_ref, pkp_ref, o_ref, *, layer):
    n = pl.program_id(1)
    blk = SWA_BLOCK
    pq = pq_ref[...]
    pk = jnp.concatenate([pkp_ref[0], pkc_ref[0]], axis=1)
    qi = lax.broadcasted_iota(jnp.int32, (blk, 2 * blk), 0)
    kj = lax.broadcasted_iota(jnp.int32, (blk, 2 * blk), 1)
    diff = qi + blk - kj
    valid = (diff >= 0) & (diff < blk) & ((n > 0) | (kj >= blk))
    neg_dist = jnp.where(valid, -jnp.abs(pq - pk).astype(F32), -jnp.inf)
    low = lax.broadcasted_iota(jnp.int32, (1, LANES), 1) < SWA_HEAD_DIM

    for t in range(SWA_KV_HEADS // 2):
        cols = slice(t * LANES, (t + 1) * LANES)
        k2 = jnp.concatenate([kp_ref[:, cols], kc_ref[:, cols]], axis=0)
        v2 = jnp.concatenate([vp_ref[:, cols], vc_ref[:, cols]], axis=0)
        qs = jnp.concatenate(
            [q_ref[:, (4 * t + i) * LANES:(4 * t + i + 1) * LANES] for i in range(SWA_GROUP)], axis=0)
        outs = []
        for half in range(2):
            g = 2 * t + half
            keep = low if half == 0 else jnp.logical_not(low)
            qm = jnp.where(keep, qs, jnp.zeros_like(qs))
            s_all = lax.dot_general(qm, k2, NT_DIMS, preferred_element_type=F32)
            probs, inv_den = [], []
            for i in range(SWA_GROUP):
                hq = SWA_GROUP * g + i
                slope2 = float(2.0 ** (-8.0 * (hq + 1) / SWA_Q_HEADS)) * LOG2E
                s = s_all[i * blk:(i + 1) * blk] + slope2 * neg_dist
                sink2 = sink_ref[layer, hq] * LOG2E
                m = jnp.maximum(jnp.max(s, axis=-1, keepdims=True), sink2)
                e = jnp.exp2(s - m)
                den = jnp.sum(e, axis=-1, keepdims=True) + jnp.exp2(sink2 - m)
                probs.append(e.astype(BF16))
                inv_den.append(1.0 / den)
            p = jnp.concatenate(probs, axis=0)
            o = jnp.dot(p, v2, preferred_element_type=F32)
            outs.append(jnp.concatenate(
                [o[i * blk:(i + 1) * blk] * inv_den[i] for i in range(SWA_GROUP)], axis=0))
        merged = jnp.where(low, outs[0], outs[1])
        for i in range(SWA_GROUP):
            o_ref[:, (4 * t + i) * LANES:(4 * t + i + 1) * LANES] = merged[i * blk:(i + 1) * blk]


def _swa(sinks, qa, ka, va, pos_col, pos_row, layer, *, B, S):
    T = qa.shape[0]
    blk = SWA_BLOCK
    nb = S // blk
    nq = SWA_Q_HEADS * SWA_HEAD_DIM
    nkv = SWA_KV_HEADS * SWA_HEAD_DIM
    cur = lambda b, n: (b * nb + n, 0)
    prev = lambda b, n: (b * nb + jnp.maximum(n - 1, 0), 0)
    cur3 = lambda b, n: (b * nb + n, 0, 0)
    prev3 = lambda b, n: (b * nb + jnp.maximum(n - 1, 0), 0, 0)
    return pl.pallas_call(
        functools.partial(_swa_kernel, layer=layer),
        grid=(B, nb),
        in_specs=[
            pl.BlockSpec(memory_space=pltpu.SMEM),
            pl.BlockSpec((blk, nq), cur),
            pl.BlockSpec((blk, nkv), cur),
            pl.BlockSpec((blk, nkv), prev),
            pl.BlockSpec((blk, nkv), cur),
            pl.BlockSpec((blk, nkv), prev),
            pl.BlockSpec((blk, 1), cur),
            pl.BlockSpec((1, 1, blk), cur3),
            pl.BlockSpec((1, 1, blk), prev3),
        ],
        out_specs=pl.BlockSpec((blk, nq), cur),
        out_shape=jax.ShapeDtypeStruct((T, nq), F32),
        compiler_params=_params(("parallel", "arbitrary")),
        name="swa_attn",
    )(sinks, qa, ka, ka, va, va, pos_col, pos_row, pos_row)


MLA_TILE = 1024
MLA_SUB = 256
MLA_HPS = 2


def _softmax_chain(s_ref, rows, nkeys, diag, m_ref, l_ref, acc_ref):
    blocks = [s_ref[rows, b * LANES:(b + 1) * LANES] for b in range(nkeys // LANES)]
    if diag:
        row = lax.broadcasted_iota(jnp.int32, (MLA_SUB, LANES), 0)
        lane = lax.broadcasted_iota(jnp.int32, (MLA_SUB, LANES), 1)
        nd = MLA_SUB // LANES
        for d in range(nd):
            b = len(blocks) - nd + d
            blocks[b] = jnp.where(lane + d * LANES <= row, blocks[b], -jnp.inf)
    mx = functools.reduce(jnp.maximum, blocks)
    m_prev = m_ref[rows, :]
    m_new = jnp.maximum(m_prev, jnp.max(mx, axis=-1, keepdims=True))
    alpha = jnp.exp2(m_prev - m_new)
    ps = [jnp.exp2(b - m_new) for b in blocks]
    l_ref[rows, :] = alpha * l_ref[rows, :] + functools.reduce(jnp.add, ps)
    acc_ref[rows, :] = alpha * acc_ref[rows, :]
    m_ref[rows, :] = m_new
    return jnp.concatenate([pb.astype(BF16) for pb in ps], axis=-1)


def _mla_kernel(q_ref, k_ref, v_ref, o_ref, s_ref, m_ref, l_ref, acc_ref):
    i = pl.program_id(2)
    heads = range(MLA_HPS)
    nchain = MLA_TILE // MLA_SUB
    rows = [slice(c * MLA_SUB, (c + 1) * MLA_SUB) for c in range(nchain)]
    m_ref[...] = jnp.full(m_ref.shape, -jnp.inf, F32)
    l_ref[...] = jnp.zeros(l_ref.shape, F32)
    acc_ref[...] = jnp.zeros(acc_ref.shape, F32)

    def chunk_rows(chunk):
        return pl.ds(pl.multiple_of(chunk * MLA_TILE, MLA_TILE), MLA_TILE)

    def scores(h, chunk, buf):
        k = k_ref[0, h, chunk_rows(chunk), :]
        s_ref[h, buf] = lax.dot_general(q_ref[0, h], k, NT_DIMS, preferred_element_type=F32)

    def state(h):
        return m_ref.at[h], l_ref.at[h], acc_ref.at[h]

    def full_chunk(h, chunk, buf):
        p = jnp.concatenate(
            [_softmax_chain(s_ref.at[h, buf], rows[c], MLA_TILE, False, *state(h)) for c in range(nchain)], axis=0)
        acc_ref[h] += jnp.dot(p, v_ref[0, h, chunk_rows(chunk), :], preferred_element_type=F32)

    def diag_chunk(h, buf):
        base = pl.multiple_of(i * MLA_TILE, MLA_TILE)
        for c in range(nchain):
            nkeys = (c + 1) * MLA_SUB
            p = _softmax_chain(s_ref.at[h, buf], rows[c], nkeys, True, *state(h))
            acc_ref[h, rows[c], :] += jnp.dot(p, v_ref[0, h, pl.ds(base, nkeys), :], preferred_element_type=F32)

    for h in heads:
        scores(h, 0, 0)

    def pair(jj, carry):
        for h in heads:
            scores(h, 2 * jj + 1, 1)
        for h in heads:
            full_chunk(h, 2 * jj, 0)
        for h in heads:
            scores(h, 2 * jj + 2, 0)
        for h in heads:
            full_chunk(h, 2 * jj + 1, 1)
        return carry

    lax.fori_loop(0, i // 2, pair, 0)

    @pl.when(i % 2 == 0)
    def _():
        for h in heads:
            diag_chunk(h, 0)

    @pl.when(i % 2 == 1)
    def _():
        for h in heads:
            scores(h, i, 1)
        for h in heads:
            full_chunk(h, i - 1, 0)
        for h in heads:
            diag_chunk(h, 1)

    for h in heads:
        l = jnp.sum(l_ref[h], axis=-1, keepdims=True)
        o_ref[0, :, h * MLA_V:(h + 1) * MLA_V] = acc_ref[h] / l


def _mla(qb, kb, vb):
    B, H, S, _ = qb.shape
    assert S % MLA_TILE == 0 and MLA_TILE % MLA_SUB == 0 and H % MLA_HPS == 0
    hp = MLA_HPS
    return pl.pallas_call(
        _mla_kernel,
        grid=(B, H // hp, S // MLA_TILE),
        in_specs=[
            pl.BlockSpec((1, hp, MLA_TILE, MLA_QK_PAD), lambda b, g, i: (b, g, i, 0)),
            pl.BlockSpec((1, hp, S, MLA_QK_PAD), lambda b, g, i: (b, g, 0, 0)),
            pl.BlockSpec((1, hp, S, MLA_V), lambda b, g, i: (b, g, 0, 0)),
        ],
        out_specs=pl.BlockSpec((1, MLA_TILE, hp * MLA_V), lambda b, g, i: (b, i, g)),
        out_shape=jax.ShapeDtypeStruct((B, S, H * MLA_V), F32),
        scratch_shapes=[pltpu.VMEM((hp, 2, MLA_TILE, MLA_TILE), F32),
                        pltpu.VMEM((hp, MLA_TILE, LANES), F32), pltpu.VMEM((hp, MLA_TILE, LANES), F32),
                        pltpu.VMEM((hp, MLA_TILE, MLA_V), F32)],
        compiler_params=_params(("parallel", "parallel", "arbitrary")),
        name="mla_attn",
    )(qb, kb, vb)


def _outproj_kernel(h_ref, oa_ref, ob_ref, ga_ref, gb_ref, woa_ref, wob_ref, o_ref):
    a = _rms(oa_ref[...], ga_ref[...]).astype(BF16)
    b = _rms(ob_ref[...], gb_ref[...]).astype(BF16)
    half = o_ref.shape[1] // 2
    for c in range(2):
        cols = slice(c * half, (c + 1) * half)
        o_ref[:, cols] = (h_ref[:, cols] + jnp.dot(a, woa_ref[:, cols], preferred_element_type=F32)
                          + jnp.dot(b, wob_ref[:, cols], preferred_element_type=F32))


def _outproj(h, oa, ob, ga, gb, woa, wob, layer, *, tm=512):
    T, D = h.shape
    row = lambda w: pl.BlockSpec((tm, w), lambda i: (i, 0))
    full = lambda a: _layer_spec(a, layer)
    return pl.pallas_call(
        _outproj_kernel,
        grid=(T // tm,),
        in_specs=[row(D), row(oa.shape[1]), row(ob.shape[1]), full(ga), full(gb), full(woa), full(wob)],
        out_specs=row(D),
        out_shape=jax.ShapeDtypeStruct((T, D), F32),
        compiler_params=_params(("parallel",)),
        name="outproj",
    )(h, oa, ob, ga, gb, woa, wob)


def _ple_kernel(h_ref, p_ref, gg_ref, wgate_ref, wp_ref, gp_ref, o_ref):
    h = h_ref[...]
    pe = _rms(jnp.dot(p_ref[...].astype(BF16), wp_ref[...], preferred_element_type=F32), gp_ref[...])
    n = _rms(h, gg_ref[...]).astype(BF16)
    half = wgate_ref.shape[1] // 2
    for c in range(2):
        cols = slice(c * half, (c + 1) * half)
        gl = jnp.dot(n, wgate_ref[:, cols], preferred_element_type=F32)
        o_ref[:, cols] = h[:, cols] + pe[:, cols] / (1.0 + jnp.exp(-gl))


def _ple(h, p, gg, wgate, wp, gp, layer, *, tm=1024):
    T, D = h.shape
    row = lambda w: pl.BlockSpec((tm, w), lambda i: (i, 0))
    full = lambda a: _layer_spec(a, layer)
    return pl.pallas_call(
        _ple_kernel,
        grid=(T // tm,),
        in_specs=[row(D), pl.BlockSpec((None, tm, p.shape[2]), lambda i: (layer, i, 0)),
                  full(gg), full(wgate), full(wp), full(gp)],
        out_specs=row(D),
        out_shape=jax.ShapeDtypeStruct((T, D), F32),
        compiler_params=_params(("parallel",)),
        name="ple",
    )(h, p, gg, wgate, wp, gp)


def _pad_rope(a):
    half = MLA_ROPE // 2
    zeros = jnp.zeros(a.shape[:-1] + (LANES // 2 - half,), a.dtype)
    return jnp.concatenate([a[..., :half], zeros, a[..., half:], zeros], axis=-1)


def _pair_heads(a, axis):
    heads = [8 * t + 4 * half + i for t in range(SWA_KV_HEADS // 2) for i in range(SWA_GROUP) for half in range(2)]
    return jnp.concatenate(
        [lax.slice_in_dim(a, SWA_HEAD_DIM * h, SWA_HEAD_DIM * (h + 1), axis=axis) for h in heads], axis=axis)


def kernel(x, p, positions, ffn1_norm, ffn1_w_gate, ffn1_w_up, ffn1_w_down, mix_norm, w_in, swa_q_norm, swa_k_norm, swa_sinks, mla_q_lora_norm, mla_w_uq, mla_kv_lora_norm, mla_w_ukv, mla_q_norm, mla_k_norm, out_norm_swa, out_norm_mla, w_o, ffn2_norm, ffn2_w_gate, ffn2_w_up, ffn2_w_down, ple_proj, ple_proj_norm, ple_gate_norm, ple_gate):
    B, S, D = x.shape
    depth = p.shape[0]
    T = B * S
    nq = SWA_Q_HEADS * SWA_HEAD_DIM
    nkv = SWA_KV_HEADS * SWA_HEAD_DIM
    o_kr = nq + 2 * nkv + MLA_Q_RANK + MLA_KV_RANK

    w1g, w1u, w1d = ffn1_w_gate, ffn1_w_up, ffn1_w_down
    w2g, w2u, w2d = ffn2_w_gate, ffn2_w_up, ffn2_w_down
    win = jnp.concatenate(
        [_pair_heads(w_in[..., :nq], -1), w_in[..., nq:o_kr], _pad_rope(w_in[..., o_kr:])], axis=-1).astype(BF16)
    wuq = jnp.concatenate(
        [mla_w_uq[..., MLA_QK * h:MLA_QK * h + MLA_NOPE] for h in range(MLA_HEADS)]
        + [_pad_rope(mla_w_uq[..., MLA_QK * h + MLA_NOPE:MLA_QK * (h + 1)]) for h in range(MLA_HEADS)],
        axis=-1).astype(BF16)
    wukv = mla_w_ukv.astype(BF16)
    woa = _pair_heads(w_o[:, :nq], 1).astype(BF16)
    wob = w_o[:, nq:].astype(BF16)
    wgate = ple_gate.astype(BF16)
    wp = ple_proj.astype(BF16)

    row = lambda v: v.reshape(depth, 1, -1)
    gqa = row(jnp.tile(swa_q_norm, (1, 2)))
    gka = row(jnp.tile(swa_k_norm, (1, 2)))
    gqn, gqr = row(mla_q_norm[:, :MLA_NOPE]), row(_pad_rope(mla_q_norm[:, MLA_NOPE:]))
    gkn, gkr = row(mla_k_norm[:, :MLA_NOPE]), row(_pad_rope(mla_k_norm[:, MLA_NOPE:]))
    ga, gb = row(_pair_heads(out_norm_swa, 1)), row(out_norm_mla)
    g1, g2, gmix = row(ffn1_norm), row(ffn2_norm), row(mix_norm)
    gcq, gckv = row(mla_q_lora_norm), row(mla_kv_lora_norm)
    gpg, gpp = row(ple_gate_norm), row(ple_proj_norm)

    half = MLA_ROPE // 2
    inv_freq = ROPE_THETA ** (-jnp.arange(half, dtype=F32) / half)
    inv_full = _pad_rope(jnp.concatenate([inv_freq, inv_freq])).reshape(1, LANES)
    sign_full = _pad_rope(jnp.concatenate([-jnp.ones((half,), F32), jnp.ones((half,), F32)])).reshape(1, LANES)
    pos_col = positions.reshape(T, 1)
    pos_row = positions.reshape(T // SWA_BLOCK, 1, SWA_BLOCK)
    cos_t, sin_t = _rope_tables(pos_col, inv_full, sign_full)

    pf = p.reshape(depth, T, -1)
    xf = x.reshape(T, D)
    for i in range(depth):
        h = _ffn(xf, g1, w1g, w1u, w1d, i)
        qa, ka, va, qb, kb, vb = _inproj(h, gmix, win, gqa, gka, gcq, wuq, gckv, wukv, gqn, gqr, gkn, gkr,
                                         cos_t, sin_t, i, B=B, S=S)
        oa = _swa(swa_sinks, qa, ka, va, pos_col, pos_row, i, B=B, S=S)
        ob = _mla(qb, kb, vb).reshape(T, MLA_HEADS * MLA_V)
        h = _outproj(h, oa, ob, ga, gb, woa, wob, i)
        h = _ffn(h, g2, w2g, w2u, w2d, i)
        xf = _ple(h, pf, gpg, wgate, wp, gpp, i)
    return xf.reshape(B, S, D)
```
